```python
import math
import jax, jax.numpy as jnp
from jax import lax
import numpy as np

D_MODEL = 1024
BATCH = 4
SEQ = 8192
DEPTH = 1

ATTN_HEADS = 8
ATTN_QK_DIM = 32
ATTN_V_DIM = 2 * ATTN_QK_DIM
ATTN_WIDTH = ATTN_HEADS * ATTN_V_DIM
Q_BLOCK = 128
SSM_HEADS = 16
SSM_HEAD_DIM = 64
SSM_WIDTH = SSM_HEADS * SSM_HEAD_DIM
SSM_GROUPS = 2
SSM_HEADS_PER_GROUP = SSM_HEADS // SSM_GROUPS
SSM_STATE = 128
CONV_WIDTH = 4
CHUNK = 128
MIX_WIDTH = ATTN_WIDTH + SSM_WIDTH
D_FF = 2816
EPS = 1e-6
Q_COLS = ATTN_HEADS * 2 * ATTN_QK_DIM
K_COLS = ATTN_HEADS * 2 * ATTN_QK_DIM
V_COLS = ATTN_WIDTH
Z_COLS = SSM_WIDTH
XBC_COLS = SSM_WIDTH + 2 * SSM_GROUPS * SSM_STATE
DT_COLS = SSM_HEADS
IN_COLS = Q_COLS + K_COLS + V_COLS + Z_COLS + XBC_COLS + DT_COLS
IN_SPLITS = (Q_COLS, Q_COLS + K_COLS, Q_COLS + K_COLS + V_COLS,
             Q_COLS + K_COLS + V_COLS + Z_COLS,
             Q_COLS + K_COLS + V_COLS + Z_COLS + XBC_COLS)

kernel_name = 'hymba_diffattn_ssd_macaron_layer'


def rms_norm(x, g):
    xf = x.astype(jnp.float32)
    y = xf * lax.rsqrt(jnp.mean(xf * xf, axis=-1, keepdims=True) + EPS)
    return (y * g.astype(jnp.float32)).astype(x.dtype)


def swiglu(h, w_gate, w_up, w_down):
    return (jax.nn.silu(h @ w_gate) * (h @ w_up)) @ w_down


def alibi_slopes(n):
    return jnp.asarray(2.0 ** (-8.0 * np.arange(1, n + 1) / n), dtype=jnp.float32)


def diff_attention(q, k, v, lam, subln_g, lambda_init):
    b, s, _ = q.shape
    f32 = jnp.float32
    q = q.astype(f32).reshape(b, s, ATTN_HEADS, 2, ATTN_QK_DIM) * (ATTN_QK_DIM ** -0.5)
    k = k.astype(f32).reshape(b, s, ATTN_HEADS, 2, ATTN_QK_DIM)
    v = v.astype(f32).reshape(b, s, ATTN_HEADS, ATTN_V_DIM).transpose(0, 2, 1, 3)
    q1, q2 = q[..., 0, :].transpose(0, 2, 1, 3), q[..., 1, :].transpose(0, 2, 1, 3)
    k1, k2 = k[..., 0, :].transpose(0, 2, 1, 3), k[..., 1, :].transpose(0, 2, 1, 3)
    slopes = alibi_slopes(ATTN_HEADS)
    kpos = jnp.arange(s)
    nb = s // Q_BLOCK

    def to_blocks(t):
        return t.reshape(b, ATTN_HEADS, nb, Q_BLOCK, ATTN_QK_DIM).transpose(2, 0, 1, 3, 4)

    def block(args):
        q1b, q2b, start = args
        qpos = start + jnp.arange(Q_BLOCK)
        dist = qpos[:, None] - kpos[None, :]
        causal = dist >= 0
        bias = -slopes[:, None, None] * dist.astype(f32)

        def probs(qb, kk):
            sc = jnp.einsum('bhqd,bhkd->bhqk', qb, kk) + bias
            return jax.nn.softmax(jnp.where(causal, sc, -jnp.inf), axis=-1)

        a = probs(q1b, k1) - lam * probs(q2b, k2)
        return jnp.einsum('bhqk,bhkv->bhqv', a, v)

    starts = jnp.arange(nb) * Q_BLOCK
    o = lax.map(block, (to_blocks(q1), to_blocks(q2), starts))
    o = o.transpose(1, 0, 3, 2, 4).reshape(b, s, ATTN_HEADS, ATTN_V_DIM)
    o = rms_norm(o, subln_g) * (1.0 - lambda_init)
    return o.reshape(b, s, ATTN_WIDTH)


def causal_depthwise_conv(u, w, bias):
    c = u.shape[-1]
    out = lax.conv_general_dilated(u, w[:, None, :], window_strides=(1,),
                                   padding=[(CONV_WIDTH - 1, 0)],
                                   dimension_numbers=('NWC', 'WIO', 'NWC'),
                                   feature_group_count=c)
    return out + bias


def segsum(a):
    t = a.shape[-1]
    rep = jnp.broadcast_to(a[..., :, None], a.shape + (t,))
    rep = jnp.where(jnp.tril(jnp.ones((t, t), dtype=bool), -1), rep, 0.0)
    ss = jnp.cumsum(rep, axis=-2)
    return jnp.where(jnp.tril(jnp.ones((t, t), dtype=bool)), ss, -jnp.inf)


def ssd_chunked(x, a, bm, cm):
    b, s = x.shape[:2]
    c = s // CHUNK
    x = x.reshape(b, c, CHUNK, SSM_GROUPS, SSM_HEADS_PER_GROUP, SSM_HEAD_DIM)
    a = a.reshape(b, c, CHUNK, SSM_GROUPS, SSM_HEADS_PER_GROUP).transpose(0, 3, 4, 1, 2)
    bm = bm.reshape(b, c, CHUNK, SSM_GROUPS, SSM_STATE)
    cm = cm.reshape(b, c, CHUNK, SSM_GROUPS, SSM_STATE)
    a_cs = jnp.cumsum(a, axis=-1)
    decay = jnp.exp(segsum(a))
    cb = jnp.einsum('bclgn,bcsgn->bgcls', cm, bm)
    y_diag = jnp.einsum('bgjcls,bcsgjp->bclgjp', cb[:, :, None] * decay, x)
    decay_states = jnp.exp(a_cs[..., -1:] - a_cs)
    states = jnp.einsum('bclgn,bgjcl,bclgjp->bcgjpn', bm, decay_states, x)
    states = jnp.concatenate([jnp.zeros_like(states[:, :1]), states], axis=1)
    decay_chunk = jnp.exp(segsum(jnp.pad(a_cs[..., -1], ((0, 0), (0, 0), (0, 0), (1, 0)))))
    states = jnp.einsum('bgjzc,bcgjpn->bzgjpn', decay_chunk, states)[:, :-1]
    y_off = jnp.einsum('bclgn,bcgjpn,bgjcl->bclgjp', cm, states, jnp.exp(a_cs))
    return (y_diag + y_off).reshape(b, s, SSM_GROUPS, SSM_HEADS_PER_GROUP, SSM_HEAD_DIM)


def ssd_mixer(z, xbc, dt, conv_w, conv_b, dt_bias, a_log, d_skip, norm_g):
    b, s, _ = z.shape
    f32 = jnp.float32
    xbc = jax.nn.silu(causal_depthwise_conv(xbc, conv_w, conv_b))
    xs, bm, cm = jnp.split(xbc, [SSM_WIDTH, SSM_WIDTH + SSM_GROUPS * SSM_STATE], axis=-1)
    xs = xs.astype(f32).reshape(b, s, SSM_GROUPS, SSM_HEADS_PER_GROUP, SSM_HEAD_DIM)
    bm = bm.astype(f32).reshape(b, s, SSM_GROUPS, SSM_STATE)
    cm = cm.astype(f32).reshape(b, s, SSM_GROUPS, SSM_STATE)
    dt = jax.nn.softplus(dt.astype(f32) + dt_bias.astype(f32)).reshape(b, s, SSM_GROUPS, SSM_HEADS_PER_GROUP)
    a = -jnp.exp(a_log.astype(f32)).reshape(SSM_GROUPS, SSM_HEADS_PER_GROUP)
    y = ssd_chunked(xs * dt[..., None], a * dt, bm, cm)
    y = y + xs * d_skip.astype(f32).reshape(SSM_GROUPS, SSM_HEADS_PER_GROUP, 1)
    y = y.reshape(b, s, SSM_WIDTH) * jax.nn.silu(z.astype(f32))
    y = rms_norm(y.reshape(b, s, SSM_GROUPS, SSM_WIDTH // SSM_GROUPS),
                 norm_g.reshape(SSM_GROUPS, SSM_WIDTH // SSM_GROUPS))
    return y.reshape(b, s, SSM_WIDTH).astype(z.dtype)


def hybrid_layer(x, ffn1_pre_g, ffn1_w_gate, ffn1_w_up, ffn1_w_down, ffn1_post_g,
                 mix_pre_g, w_in, lambda_q1, lambda_k1, lambda_q2, lambda_k2, attn_subln_g,
                 conv_w, conv_b, dt_bias, a_log, d_skip, ssm_norm_g, w_out, mix_post_g,
                 ffn2_pre_g, ffn2_w_gate, ffn2_w_up, ffn2_w_down, ffn2_post_g, lambda_init):
    h = rms_norm(x, ffn1_pre_g)
    x = x + 0.5 * rms_norm(swiglu(h, ffn1_w_gate, ffn1_w_up, ffn1_w_down), ffn1_post_g)
    h = rms_norm(x, mix_pre_g)
    proj = h @ w_in
    q, k, v, z, xbc, dt = jnp.split(proj, IN_SPLITS, axis=-1)
    f32 = jnp.float32
    lam = (jnp.exp(jnp.sum(lambda_q1.astype(f32) * lambda_k1.astype(f32)))
           - jnp.exp(jnp.sum(lambda_q2.astype(f32) * lambda_k2.astype(f32))) + lambda_init)
    attn = diff_attention(q, k, v, lam, attn_subln_g, lambda_init).astype(x.dtype)
    ssm = ssd_mixer(z, xbc, dt, conv_w, conv_b, dt_bias, a_log, d_skip, ssm_norm_g)
    mixed = jnp.concatenate([attn, ssm], axis=-1) @ w_out
    x = x + rms_norm(mixed, mix_post_g)
    h = rms_norm(x, ffn2_pre_g)
    x = x + 0.5 * rms_norm(swiglu(h, ffn2_w_gate, ffn2_w_up, ffn2_w_down), ffn2_post_g)
    return x


def setup_inputs(seed: int = 0) -> dict:
    key = jax.random.key(seed)
    ks = jax.random.split(key, 28)
    f32 = jnp.float32

    def normal(k, shape, scale):
        return jax.random.normal(k, shape, f32) * scale

    def gain(k, shape):
        return 1.0 + 0.05 * jax.random.normal(k, shape, f32)

    L = DEPTH
    u = jax.random.uniform(ks[15], (L, SSM_HEADS), f32)
    dt0 = jnp.exp(u * (math.log(0.1) - math.log(0.001)) + math.log(0.001))
    dt_bias = dt0 + jnp.log(-jnp.expm1(-dt0))
    return {
        'x': normal(ks[0], (BATCH, SEQ, D_MODEL), 1.0),
        'ffn1_pre_g': gain(ks[1], (L, D_MODEL)),
        'ffn1_w_gate': normal(ks[2], (L, D_MODEL, D_FF), D_MODEL ** -0.5),
        'ffn1_w_up': normal(ks[3], (L, D_MODEL, D_FF), D_MODEL ** -0.5),
        'ffn1_w_down': normal(ks[4], (L, D_FF, D_MODEL), D_FF ** -0.5),
        'ffn1_post_g': gain(ks[5], (L, D_MODEL)),
        'mix_pre_g': gain(ks[6], (L, D_MODEL)),
        'w_in': normal(ks[7], (L, D_MODEL, IN_COLS), D_MODEL ** -0.5),
        'lambda_q1': normal(ks[8], (L, ATTN_QK_DIM), 0.1),
        'lambda_k1': normal(ks[9], (L, ATTN_QK_DIM), 0.1),
        'lambda_q2': normal(ks[10], (L, ATTN_QK_DIM), 0.1),
        'lambda_k2': normal(ks[11], (L, ATTN_QK_DIM), 0.1),
        'attn_subln_g': gain(ks[12], (L, ATTN_V_DIM)),
        'conv_w': normal(ks[13], (L, CONV_WIDTH, XBC_COLS), CONV_WIDTH ** -0.5),
        'conv_b': normal(ks[14], (L, XBC_COLS), 0.02),
        'dt_bias': dt_bias,
        'a_log': jnp.log(jax.random.uniform(ks[16], (L, SSM_HEADS), f32, 1.0, 16.0)),
        'd_skip': 1.0 + 0.1 * jax.random.normal(ks[17], (L, SSM_HEADS), f32),
        'ssm_norm_g': gain(ks[18], (L, SSM_WIDTH)),
        'w_out': normal(ks[19], (L, MIX_WIDTH, D_MODEL), MIX_WIDTH ** -0.5),
        'mix_post_g': gain(ks[20], (L, D_MODEL)),
        'ffn2_pre_g': gain(ks[21], (L, D_MODEL)),
        'ffn2_w_gate': normal(ks[22], (L, D_MODEL, D_FF), D_MODEL ** -0.5),
        'ffn2_w_up': normal(ks[23], (L, D_MODEL, D_FF), D_MODEL ** -0.5),
        'ffn2_w_down': normal(ks[24], (L, D_FF, D_MODEL), D_FF ** -0.5),
        'ffn2_post_g': gain(ks[25], (L, D_MODEL)),
    }


def reference(x, ffn1_pre_g, ffn1_w_gate, ffn1_w_up, ffn1_w_down, ffn1_post_g,
              mix_pre_g, w_in, lambda_q1, lambda_k1, lambda_q2, lambda_k2, attn_subln_g,
              conv_w, conv_b, dt_bias, a_log, d_skip, ssm_norm_g, w_out, mix_post_g,
              ffn2_pre_g, ffn2_w_gate, ffn2_w_up, ffn2_w_down, ffn2_post_g):
    for i in range(DEPTH):
        lambda_init = 0.8 - 0.6 * math.exp(-0.3 * i)
        x = hybrid_layer(x, ffn1_pre_g[i], ffn1_w_gate[i], ffn1_w_up[i], ffn1_w_down[i], ffn1_post_g[i],
                         mix_pre_g[i], w_in[i], lambda_q1[i], lambda_k1[i], lambda_q2[i], lambda_k2[i],
                         attn_subln_g[i], conv_w[i], conv_b[i], dt_bias[i], a_log[i], d_skip[i],
                         ssm_norm_g[i], w_out[i], mix_post_g[i],
                         ffn2_pre_g[i], ffn2_w_gate[i], ffn2_w_up[i], ffn2_w_down[i], ffn2_post_g[i],
                         lambda_init)
    return x
```

```python
import functools
import math

import jax
import jax.numpy as jnp
from jax import lax
from jax.experimental import pallas as pl
from jax.experimental.pallas import tpu as pltpu

F32 = jnp.float32
BF16 = jnp.bfloat16

EPS = 1e-6
LANES = 128
ATTN_HEADS = 8
ATTN_QK_DIM = 32
ATTN_V_DIM = 64
ATTN_WIDTH = ATTN_HEADS * ATTN_V_DIM
SSM_HEADS = 16
SSM_HEAD_DIM = 64
SSM_WIDTH = SSM_HEADS * SSM_HEAD_DIM
SSM_GROUPS = 2
SSM_STATE = 128
CONV_WIDTH = 4
XBC_COLS = SSM_WIDTH + 2 * SSM_GROUPS * SSM_STATE
GROUP_WIDTH = SSM_WIDTH // SSM_GROUPS
NEG_BIG = -1e30

VMEM_LIMIT_BYTES = 56 * 1024 * 1024

FFN_TOKEN_TILE = 512
FFN_FF_CHUNK = 256
PROJ_TOKEN_TILE = 512
ATTN_TILE = 512
SSD_CHUNK = 128
CONV_HALO = 8


def _rms(xf, g):
    return xf * lax.rsqrt(jnp.mean(xf * xf, axis=-1, keepdims=True) + EPS) * g


def _silu(v):
    return v / (1.0 + jnp.exp(-v))


def _params(*semantics):
    return pltpu.CompilerParams(dimension_semantics=semantics, vmem_limit_bytes=VMEM_LIMIT_BYTES)


def _const_spec(shape):
    return pl.BlockSpec(shape, lambda *_: (0,) * len(shape))


def _ffn_body(x_ref, pre_g_ref, wg_ref, wu_ref, wd_ref, post_g_ref, o_ref):
    x = x_ref[...]
    h = _rms(x, pre_g_ref[...]).astype(BF16)
    d_ff = wg_ref.shape[1]
    y = jnp.zeros(x.shape, F32)
    for c in range(d_ff // FFN_FF_CHUNK):
        cols = slice(c * FFN_FF_CHUNK, (c + 1) * FFN_FF_CHUNK)
        g = jnp.dot(h, wg_ref[:, cols], preferred_element_type=F32)
        u = jnp.dot(h, wu_ref[:, cols], preferred_element_type=F32)
        act = (_silu(g) * u).astype(BF16)
        y = y + jnp.dot(act, wd_ref[cols, :], preferred_element_type=F32)
    o_ref[...] = x + 0.5 * _rms(y, post_g_ref[...])


def _ffn(x, pre_g, wg, wu, wd, post_g):
    t, d = x.shape
    d_ff = wg.shape[1]
    tm = FFN_TOKEN_TILE
    return pl.pallas_call(
        _ffn_body,
        out_shape=jax.ShapeDtypeStruct((t, d), F32),
        grid=(t // tm,),
        in_specs=[
            pl.BlockSpec((tm, d), lambda i: (i, 0)),
            _const_spec((1, d)),
            _const_spec((d, d_ff)),
            _const_spec((d, d_ff)),
            _const_spec((d_ff, d)),
            _const_spec((1, d)),
        ],
        out_specs=pl.BlockSpec((tm, d), lambda i: (i, 0)),
        compiler_params=_params("parallel"),
        name="ffn",
    )(x, pre_g, wg, wu, wd, post_g)


def _inproj_body(x_ref, g_ref, w_ref, q_ref, k_ref, v_ref, z_ref, xbc_ref, dt_ref):
    h = _rms(x_ref[...], g_ref[...]).astype(BF16)
    col = 0

    def proj(width):
        nonlocal col
        out = jnp.dot(h, w_ref[:, col:col + width], preferred_element_type=F32)
        col += width
        return out

    q_ref[...] = (proj(ATTN_WIDTH) * (ATTN_QK_DIM ** -0.5)).astype(BF16)
    k_ref[...] = proj(ATTN_WIDTH).astype(BF16)
    v_ref[...] = proj(ATTN_WIDTH).astype(BF16)
    z_ref[...] = proj(SSM_WIDTH)
    xbc_ref[...] = proj(XBC_COLS)
    dt_ref[...] = proj(LANES)


def _inproj(x, g, w):
    t, d = x.shape
    tm = PROJ_TOKEN_TILE
    widths = (ATTN_WIDTH, ATTN_WIDTH, ATTN_WIDTH, SSM_WIDTH, XBC_COLS, LANES)
    dtypes = (BF16, BF16, BF16, F32, F32, F32)
    return pl.pallas_call(
        _inproj_body,
        out_shape=[jax.ShapeDtypeStruct((t, n), dt) for n, dt in zip(widths, dtypes)],
        grid=(t // tm,),
        in_specs=[
            pl.BlockSpec((tm, d), lambda i: (i, 0)),
            _const_spec((1, d)),
            _const_spec(w.shape),
        ],
        out_specs=[pl.BlockSpec((tm, n), lambda i: (i, 0)) for n in widths],
        compiler_params=_params("parallel"),
        name="inproj",
    )(x, g, w)


def _attn_body(lam_ref, g_ref, q_ref, k_ref, v_ref, o_ref, m_sc, l_sc, acc_sc, *, lambda_init):
    tq = q_ref.shape[0]
    tk = tq
    pair = pl.program_id(1)
    i = pl.program_id(2)
    n_maps = 4

    q = q_ref[...]
    lane = lax.broadcasted_iota(jnp.int32, (1, LANES), 1)
    qm = [jnp.where((lane >= 32 * m) & (lane < 32 * (m + 1)), q, jnp.zeros_like(q)) for m in range(n_maps)]
    head_a = (2 * pair + 1).astype(F32)
    slope = [jnp.exp2(-(head_a + (m // 2))) for m in range(n_maps)]
    col_pos = lax.broadcasted_iota(jnp.int32, (1, tk), 1).astype(F32)
    causal = (lax.broadcasted_iota(jnp.int32, (tq, tk), 0) >= lax.broadcasted_iota(jnp.int32, (tq, tk), 1))

    m_sc[...] = jnp.full(m_sc.shape, NEG_BIG, F32)
    l_sc[...] = jnp.zeros(l_sc.shape, F32)
    acc_sc[...] = jnp.zeros(acc_sc.shape, F32)

    def step(j, masked):
        start = pl.multiple_of(j * tk, tk)
        kt = k_ref[pl.ds(start, tk), :]
        vt = v_ref[pl.ds(start, tk), :]
        block_off = ((j - i) * tk).astype(F32)
        for m in range(n_maps):
            s = lax.dot_general(qm[m], kt, (((1,), (1,)), ((), ())), preferred_element_type=F32)
            s = s + slope[m] * col_pos
            if masked:
                s = jnp.where(causal, s, -jnp.inf)
            c = slope[m] * block_off
            tile_max = s[:, 0:LANES]
            for t in range(1, tk // LANES):
                tile_max = jnp.maximum(tile_max, s[:, t * LANES:(t + 1) * LANES])
            row_max = jnp.max(tile_max, axis=1, keepdims=True) + c
            m_old = m_sc[m]
            m_new = jnp.maximum(m_old, row_max)
            alpha = jnp.exp(m_old - m_new)
            p = jnp.exp(s - (m_new[:, 0:1] - c))
            tile_sum = p[:, 0:LANES]
            for t in range(1, tk // LANES):
                tile_sum = tile_sum + p[:, t * LANES:(t + 1) * LANES]
            l_sc[m] = alpha * l_sc[m] + tile_sum
            acc_sc[m] = alpha * acc_sc[m] + jnp.dot(p.astype(BF16), vt, preferred_element_type=F32)
            m_sc[m] = m_new

    def unmasked(j, carry):
        step(j, False)
        return carry

    lax.fori_loop(0, i, unmasked, 0)
    step(i, True)

    lp = lam_ref[...]
    lam = (jnp.exp(jnp.sum(lp[0:1] * lp[1:2], axis=1, keepdims=True))
           - jnp.exp(jnp.sum(lp[2:3] * lp[3:4], axis=1, keepdims=True)) + lambda_init)
    o_maps = [acc_sc[m] / jnp.sum(l_sc[m], axis=1, keepdims=True) for m in range(n_maps)]
    first = lane < ATTN_V_DIM
    o = jnp.where(first, o_maps[0] - lam * o_maps[1], o_maps[2] - lam * o_maps[3])
    sq = o * o
    ss_a = jnp.sum(jnp.where(first, sq, 0.0), axis=1, keepdims=True)
    ss_b = jnp.sum(jnp.where(first, 0.0, sq), axis=1, keepdims=True)
    ms = jnp.where(first, ss_a, ss_b) * (1.0 / ATTN_V_DIM)
    o = o * lax.rsqrt(ms + EPS) * g_ref[...] * (1.0 - lambda_init)
    o_ref[...] = o.astype(o_ref.dtype)


def _attention(lam, g2, q, k, v, batch, seq, lambda_init):
    t = q.shape[0]
    tq = ATTN_TILE
    nq = seq // tq
    n_pairs = ATTN_WIDTH // LANES
    return pl.pallas_call(
        functools.partial(_attn_body, lambda_init=lambda_init),
        out_shape=jax.ShapeDtypeStruct((t, ATTN_WIDTH), BF16),
        grid=(batch, n_pairs, nq),
        in_specs=[
            _const_spec(lam.shape),
            _const_spec((1, LANES)),
            pl.BlockSpec((tq, LANES), lambda b, p, i: (b * nq + i, p)),
            pl.BlockSpec((seq, LANES), lambda b, p, i: (b, p)),
            pl.BlockSpec((seq, LANES), lambda b, p, i: (b, p)),
        ],
        out_specs=pl.BlockSpec((tq, LANES), lambda b, p, i: (b * nq + i, p)),
        scratch_shapes=[
            pltpu.VMEM((4, tq, LANES), F32),
            pltpu.VMEM((4, tq, LANES), F32),
            pltpu.VMEM((4, tq, LANES), F32),
        ],
        compiler_params=_params("parallel", "parallel", "arbitrary"),
        name="attn",
    )(lam, g2, q, k, v)


def _expand_heads(per_head):
    rows = per_head.shape[0]
    first = lax.broadcasted_iota(jnp.int32, (1, LANES), 1) < SSM_HEAD_DIM
    tiles = []
    for t in range(SSM_HEADS // 2):
        a = jnp.broadcast_to(per_head[:, 2 * t:2 * t + 1], (rows, LANES))
        b = jnp.broadcast_to(per_head[:, 2 * t + 1:2 * t + 2], (rows, LANES))
        tiles.append(jnp.where(first, a, b))
    return tiles


def _ssd_body(z_ref, xbc_ref, dt_ref, convw_ref, convb_ref, dtb_ref, alog_ref, dskip_ref, ng_ref,
              o_ref, ubuf, st_sc):
    ch = z_ref.shape[0]
    chunk = pl.program_id(1)

    @pl.when(chunk == 0)
    def _():
        ubuf[0:CONV_HALO, :] = jnp.zeros((CONV_HALO, XBC_COLS), F32)
        st_sc[...] = jnp.zeros(st_sc.shape, F32)

    ubuf[CONV_HALO:CONV_HALO + ch, :] = xbc_ref[...]
    conv = convb_ref[...]
    for tap in range(CONV_WIDTH):
        shift = CONV_WIDTH - 1 - tap
        conv = conv + convw_ref[tap:tap + 1, :] * ubuf[CONV_HALO - shift:CONV_HALO - shift + ch, :]
    ubuf[0:CONV_HALO, :] = ubuf[ch:ch + CONV_HALO, :]
    xa = _silu(conv)

    n_tiles = SSM_WIDTH // LANES
    tiles_per_group = n_tiles // SSM_GROUPS
    xs = [xa[:, t * LANES:(t + 1) * LANES] for t in range(n_tiles)]
    bm = [xa[:, SSM_WIDTH + g * SSM_STATE:SSM_WIDTH + (g + 1) * SSM_STATE].astype(BF16) for g in range(SSM_GROUPS)]
    c_off = SSM_WIDTH + SSM_GROUPS * SSM_STATE
    cm = [xa[:, c_off + g * SSM_STATE:c_off + (g + 1) * SSM_STATE].astype(BF16) for g in range(SSM_GROUPS)]

    dt_in = dt_ref[...] + dtb_ref[...]
    dt = jnp.maximum(dt_in, 0.0) + jnp.log1p(jnp.exp(-jnp.abs(dt_in)))
    da = dt * (-jnp.exp(alog_ref[...]))
    row = lax.broadcasted_iota(jnp.int32, (ch, ch), 0)
    colm = lax.broadcasted_iota(jnp.int32, (ch, ch), 1)
    tri = row >= colm
    acs = jnp.dot(tri.astype(F32), da, preferred_element_type=F32, precision=lax.Precision.HIGHEST)
    acs_t = acs.T
    acs_last = acs[ch - 1:ch, :]

    dt_x = _expand_heads(dt)
    decay_in = _expand_heads(jnp.exp(acs))
    decay_out = _expand_heads(jnp.exp(acs_last - acs))
    first = lax.broadcasted_iota(jnp.int32, (1, LANES), 1) < SSM_HEAD_DIM

    xdt = [xs[t] * dt_x[t] for t in range(n_tiles)]
    y_tiles = []
    for g in range(SSM_GROUPS):
        cb = lax.dot_general(cm[g], bm[g], (((1,), (1,)), ((), ())), preferred_element_type=F32)
        st_prev = st_sc[g]
        y_off = jnp.dot(cm[g], st_prev.astype(BF16), preferred_element_type=F32)
        xdec = []
        for tg in range(tiles_per_group):
            t = g * tiles_per_group + tg
            xdt_b = xdt[t].astype(BF16)
            halves = []
            for h in (2 * t, 2 * t + 1):
                seg = acs[:, h:h + 1] - acs_t[h:h + 1, :]
                gmat = (cb * jnp.exp(jnp.where(tri, seg, -jnp.inf))).astype(BF16)
                halves.append(jnp.dot(gmat, xdt_b, preferred_element_type=F32))
            y_diag = jnp.where(first, halves[0], halves[1])
            y_tiles.append(y_diag + y_off[:, tg * LANES:(tg + 1) * LANES] * decay_in[t]
                           + xs[t] * dskip_ref[:, t * LANES:(t + 1) * LANES])
            xdec.append((xdt[t] * decay_out[t]).astype(BF16))
        xdec_g = jnp.concatenate(xdec, axis=1)
        contrib = lax.dot_general(bm[g], xdec_g, (((0,), (0,)), ((), ())), preferred_element_type=F32)
        chunk_decay = jnp.concatenate(
            [decay_in[g * tiles_per_group + tg][ch - 1:ch, :] for tg in range(tiles_per_group)], axis=1)
        st_sc[g] = st_prev * chunk_decay + contrib

    for g in range(SSM_GROUPS):
        yg = jnp.concatenate(y_tiles[g * tiles_per_group:(g + 1) * tiles_per_group], axis=1)
        cols = slice(g * GROUP_WIDTH, (g + 1) * GROUP_WIDTH)
        yg = yg * _silu(z_ref[:, cols])
        o_ref[:, cols] = _rms(yg, ng_ref[:, cols]).astype(o_ref.dtype)


def _ssd(z, xbc, dt, conv_w, conv_b, dt_bias, a_log, d_skip, norm_g, batch, seq):
    t = z.shape[0]
    ch = SSD_CHUNK
    nc = seq // ch
    tok = lambda b, c: (b * nc + c, 0)
    return pl.pallas_call(
        _ssd_body,
        out_shape=jax.ShapeDtypeStruct((t, SSM_WIDTH), BF16),
        grid=(batch, nc),
        in_specs=[
            pl.BlockSpec((ch, SSM_WIDTH), tok),
            pl.BlockSpec((ch, XBC_COLS), tok),
            pl.BlockSpec((ch, LANES), tok),
            _const_spec((CONV_WIDTH, XBC_COLS)),
            _const_spec((1, XBC_COLS)),
            _const_spec((1, LANES)),
            _const_spec((1, LANES)),
            _const_spec((1, SSM_WIDTH)),
            _const_spec((1, SSM_WIDTH)),
        ],
        out_specs=pl.BlockSpec((ch, SSM_WIDTH), tok),
        scratch_shapes=[
            pltpu.VMEM((CONV_HALO + ch, XBC_COLS), F32),
            pltpu.VMEM((SSM_GROUPS, SSM_STATE, GROUP_WIDTH), F32),
        ],
        compiler_params=_params("parallel", "arbitrary"),
        name="ssd",
    )(z, xbc, dt, conv_w, conv_b, dt_bias, a_log, d_skip, norm_g)


def _outproj_body(x_ref, attn_ref, ssm_ref, wa_ref, ws_ref, g_ref, o_ref):
    mixed = jnp.dot(attn_ref[...], wa_ref[...], preferred_element_type=F32)
    mixed = mixed + jnp.dot(ssm_ref[...], ws_ref[...], preferred_element_type=F32)
    o_ref[...] = x_ref[...] + _rms(mixed, g_ref[...])


def _outproj(x, attn, ssm, wa, ws, g):
    t, d = x.shape
    tm = PROJ_TOKEN_TILE
    return pl.pallas_call(
        _outproj_body,
        out_shape=jax.ShapeDtypeStruct((t, d), F32),
        grid=(t // tm,),
        in_specs=[
            pl.BlockSpec((tm, d), lambda i: (i, 0)),
            pl.BlockSpec((tm, ATTN_WIDTH), lambda i: (i, 0)),
            pl.BlockSpec((tm, SSM_WIDTH), lambda i: (i, 0)),
            _const_spec(wa.shape),
            _const_spec(ws.shape),
            _const_spec((1, d)),
        ],
        out_specs=pl.BlockSpec((tm, d), lambda i: (i, 0)),
        compiler_params=_params("parallel"),
        name="outproj",
    )(x, attn, ssm, wa, ws, g)


def _pad_lanes(row):
    return jnp.pad(row.astype(F32), (0, LANES - row.shape[0]))[None, :]


def _layer(x, p, layer_index, batch, seq):
    lambda_init = 0.8 - 0.6 * math.exp(-0.3 * layer_index)
    row = lambda a: a.astype(F32)[None, :]

    x = _ffn(x, row(p["ffn1_pre_g"]), p["ffn1_w_gate"].astype(BF16), p["ffn1_w_up"].astype(BF16),
             p["ffn1_w_down"].astype(BF16), row(p["ffn1_post_g"]))

    w_in = p["w_in"]
    dt_cols = w_in[:, -SSM_HEADS:]
    w_packed = jnp.concatenate(
        [w_in[:, :-SSM_HEADS], jnp.pad(dt_cols, ((0, 0), (0, LANES - SSM_HEADS)))], axis=1).astype(BF16)
    q, k, v, z, xbc, dt = _inproj(x, row(p["mix_pre_g"]), w_packed)

    f32 = lambda a: a.astype(F32)
    lam = jnp.stack([f32(p[n]) for n in ("lambda_q1", "lambda_k1", "lambda_q2", "lambda_k2")])
    g2 = jnp.tile(row(p["attn_subln_g"]), (1, LANES // ATTN_V_DIM))
    attn = _attention(lam, g2, q, k, v, batch, seq, lambda_init)

    d_skip = jnp.repeat(f32(p["d_skip"]), SSM_HEAD_DIM)[None, :]
    ssm = _ssd(z, xbc, dt, f32(p["conv_w"]), row(p["conv_b"]), _pad_lanes(p["dt_bias"]),
               _pad_lanes(p["a_log"]), d_skip, row(p["ssm_norm_g"]), batch, seq)

    w_out = p["w_out"].astype(BF16)
    x = _outproj(x, attn, ssm, w_out[:ATTN_WIDTH], w_out[ATTN_WIDTH:], row(p["mix_post_g"]))

    return _ffn(x, row(p["ffn2_pre_g"]), p["ffn2_w_gate"].astype(BF16), p["ffn2_w_up"].astype(BF16),
                p["ffn2_w_down"].astype(BF16), row(p["ffn2_post_g"]))


def kernel(x, ffn1_pre_g, ffn1_w_gate, ffn1_w_up, ffn1_w_down, ffn1_post_g, mix_pre_g, w_in, lambda_q1, lambda_k1, lambda_q2, lambda_k2, attn_subln_g, conv_w, conv_b, dt_bias, a_log, d_skip, ssm_norm_g, w_out, mix_post_g, ffn2_pre_g, ffn2_w_gate, ffn2_w_up, ffn2_w_down, ffn2_post_g):
    params = dict(
        ffn1_pre_g=ffn1_pre_g, ffn1_w_gate=ffn1_w_gate, ffn1_w_up=ffn1_w_up, ffn1_w_down=ffn1_w_down,
        ffn1_post_g=ffn1_post_g, mix_pre_g=mix_pre_g, w_in=w_in, lambda_q1=lambda_q1, lambda_k1=lambda_k1,
        lambda_q2=lambda_q2, lambda_k2=lambda_k2, attn_subln_g=attn_subln_g, conv_w=conv_w, conv_b=conv_b,
        dt_bias=dt_bias, a_log=a_log, d_skip=d_skip, ssm_norm_g=ssm_norm_g, w_out=w_out,
        mix_post_g=mix_post_g, ffn2_pre_g=ffn2_pre_g, ffn2_w_gate=ffn2_w_gate, ffn2_w_up=ffn2_w_up,
        ffn2_w_down=ffn2_w_down, ffn2_post_g=ffn2_post_g)
    batch, seq, d = x.shape
    h = x.reshape(batch * seq, d)
    for i in range(ffn1_pre_g.shape[0]):
        h = _layer(h, {name: a[i] for name, a in params.items()}, i, batch, seq)
    return h.reshape(batch, seq, d)
```

```python
import functools
import math

import jax
import jax.numpy as jnp
from jax import lax
from jax.experimental import pallas as pl
from jax.experimental.pallas import tpu as pltpu

F32 = jnp.float32
BF16 = jnp.bfloat16

EPS = 1e-6
LANES = 128
ATTN_HEADS = 8
ATTN_QK_DIM = 32
ATTN_V_DIM = 64
ATTN_WIDTH = ATTN_HEADS * ATTN_V_DIM
SSM_HEADS = 16
SSM_HEAD_DIM = 64
SSM_WIDTH = SSM_HEADS * SSM_HEAD_DIM
SSM_GROUPS = 2
SSM_STATE = 128
CONV_WIDTH = 4
XBC_COLS = SSM_WIDTH + 2 * SSM_GROUPS * SSM_STATE
GROUP_WIDTH = SSM_WIDTH // SSM_GROUPS
NEG_BIG = -1e30

VMEM_LIMIT_BYTES = 56 * 1024 * 1024

FFN_TOKEN_TILE = 512
FFN_FF_CHUNK = 256
PROJ_TOKEN_TILE = 512
ATTN_TILE = 512
SSD_CHUNK = 128
CONV_HALO = 8


def _rms(xf, g):
    return xf * lax.rsqrt(jnp.mean(xf * xf, axis=-1, keepdims=True) + EPS) * g


def _silu(v):
    return v / (1.0 + jnp.exp(-v))


def _params(*semantics):
    return pltpu.CompilerParams(dimension_semantics=semantics, vmem_limit_bytes=VMEM_LIMIT_BYTES)


def _const_spec(shape):
    return pl.BlockSpec(shape, lambda *_: (0,) * len(shape))


def _ffn_body(x_ref, pre_g_ref, wg_ref, wu_ref, wd_ref, post_g_ref, o_ref):
    x = x_ref[...]
    h = _rms(x, pre_g_ref[...]).astype(BF16)
    d_ff = wg_ref.shape[1]
    y = jnp.zeros(x.shape, F32)
    for c in range(d_ff // FFN_FF_CHUNK):
        cols = slice(c * FFN_FF_CHUNK, (c + 1) * FFN_FF_CHUNK)
        g = jnp.dot(h, wg_ref[:, cols], preferred_element_type=F32)
        u = jnp.dot(h, wu_ref[:, cols], preferred_element_type=F32)
        act = (_silu(g) * u).astype(BF16)
        y = y + jnp.dot(act, wd_ref[cols, :], preferred_element_type=F32)
    o_ref[...] = x + 0.5 * _rms(y, post_g_ref[...])


def _ffn(x, pre_g, wg, wu, wd, post_g):
    t, d = x.shape
    d_ff = wg.shape[1]
    tm = FFN_TOKEN_TILE
    return pl.pallas_call(
        _ffn_body,
        out_shape=jax.ShapeDtypeStruct((t, d), F32),
        grid=(t // tm,),
        in_specs=[
            pl.BlockSpec((tm, d), lambda i: (i, 0)),
            _const_spec((1, d)),
            _const_spec((d, d_ff)),
            _const_spec((d, d_ff)),
            _const_spec((d_ff, d)),
            _const_spec((1, d)),
        ],
        out_specs=pl.BlockSpec((tm, d), lambda i: (i, 0)),
        compiler_params=_params("parallel"),
        name="ffn",
    )(x, pre_g, wg, wu, wd, post_g)


def _inproj_body(x_ref, g_ref, w_ref, q_ref, k_ref, v_ref, z_ref, xbc_ref, dt_ref):
    h = _rms(x_ref[...], g_ref[...]).astype(BF16)
    col = 0

    def proj(width):
        nonlocal col
        out = jnp.dot(h, w_ref[:, col:col + width], preferred_element_type=F32)
        col += width
        return out

    q_ref[...] = (proj(ATTN_WIDTH) * (ATTN_QK_DIM ** -0.5 * math.log2(math.e))).astype(BF16)
    k_ref[...] = proj(ATTN_WIDTH).astype(BF16)
    v_ref[...] = proj(ATTN_WIDTH).astype(BF16)
    z_ref[...] = proj(SSM_WIDTH)
    xbc_ref[...] = proj(XBC_COLS)
    dt_ref[...] = proj(LANES)


def _inproj(x, g, w):
    t, d = x.shape
    tm = PROJ_TOKEN_TILE
    widths = (ATTN_WIDTH, ATTN_WIDTH, ATTN_WIDTH, SSM_WIDTH, XBC_COLS, LANES)
    dtypes = (BF16, BF16, BF16, F32, F32, F32)
    return pl.pallas_call(
        _inproj_body,
        out_shape=[jax.ShapeDtypeStruct((t, n), dt) for n, dt in zip(widths, dtypes)],
        grid=(t // tm,),
        in_specs=[
            pl.BlockSpec((tm, d), lambda i: (i, 0)),
            _const_spec((1, d)),
            _const_spec(w.shape),
        ],
        out_specs=[pl.BlockSpec((tm, n), lambda i: (i, 0)) for n in widths],
        compiler_params=_params("parallel"),
        name="inproj",
    )(x, g, w)


N_MAPS = 4


def _attn_body(lam_ref, g_ref, q_ref, k_ref, v_ref, o_ref, qm_sc, s0_sc, s1_sc, mx0_sc, mx1_sc, m_sc, acc_sc, *,
               tq, lambda_init):
    seq = q_ref.shape[0]
    tk = tq
    nq = seq // tq
    n_lane_tiles = tk // LANES
    pair = pl.program_id(1)
    log2e = math.log2(math.e)

    lane = lax.broadcasted_iota(jnp.int32, (1, LANES), 1)
    first = lane < ATTN_V_DIM
    head_a = (2 * pair + 1).astype(F32)
    slope = [jnp.exp2(-(head_a + (m // 2))) * log2e for m in range(N_MAPS)]
    col_pos = lax.broadcasted_iota(jnp.int32, (1, LANES), 1).astype(F32)

    def mask_q(r, carry):
        rows = pl.ds(pl.multiple_of(r * tq, tq), tq)
        q = q_ref[rows, :]
        for m in range(N_MAPS):
            qm_sc[m, rows, :] = jnp.where((lane >= 32 * m) & (lane < 32 * (m + 1)), q, jnp.zeros_like(q))
        return carry

    lax.fori_loop(0, nq, mask_q, 0)

    def scores(row, tile, s_ref, mx_ref, diag):
        rows = pl.ds(pl.multiple_of(row * tq, tq), tq)
        kt = k_ref[pl.ds(pl.multiple_of(tile * tk, tk), tk), :]
        rel = lax.convert_element_type(tile * tk - row * tq, F32)
        for m in range(N_MAPS):
            s = lax.dot_general(qm_sc[m, rows, :], kt, (((1,), (1,)), ((), ())),
                                preferred_element_type=F32)
            tile_max = None
            for t in range(n_lane_tiles):
                st = s[:, t * LANES:(t + 1) * LANES] + slope[m] * (col_pos + (rel + t * LANES))
                if diag:
                    visible = (lax.broadcasted_iota(jnp.int32, (tq, LANES), 0)
                               >= lax.broadcasted_iota(jnp.int32, (tq, LANES), 1) + t * LANES)
                    st = jnp.where(visible, st, -jnp.inf)
                s_ref[m, :, t * LANES:(t + 1) * LANES] = st
                tile_max = st if tile_max is None else jnp.maximum(tile_max, st)
            mx_ref[m] = tile_max

    def softmax_pv(s_ref, mx_ref, tile):
        vt = v_ref[pl.ds(pl.multiple_of(tile * tk, tk), tk), :]
        ones = jnp.ones_like(vt)
        v_heads = (jnp.where(first, vt, ones), jnp.where(first, ones, vt))
        for m in range(N_MAPS):
            m_old = m_sc[m]
            m_new = jnp.maximum(m_old, jnp.max(mx_ref[m], axis=1, keepdims=True))
            m_sc[m] = m_new
            acc_sc[m] = acc_sc[m] * jnp.exp2(m_old - m_new)
        for m in range(N_MAPS):
            p = jnp.concatenate(
                [jnp.exp2(s_ref[m, :, t * LANES:(t + 1) * LANES] - m_sc[m]).astype(BF16)
                 for t in range(n_lane_tiles)], axis=1)
            acc_sc[m] += jnp.dot(p, v_heads[m // 2], preferred_element_type=F32)

    def finalize(row):
        lp = lam_ref[...]
        lam = (jnp.exp(jnp.sum(lp[0:1] * lp[1:2], axis=1, keepdims=True))
               - jnp.exp(jnp.sum(lp[2:3] * lp[3:4], axis=1, keepdims=True)) + lambda_init)
        ratio = [acc_sc[m] / pltpu.roll(acc_sc[m], ATTN_V_DIM, 1) for m in range(N_MAPS)]
        o = jnp.where(first, ratio[0] - lam * ratio[1], ratio[2] - lam * ratio[3])
        sq = o * o
        ss_a = jnp.sum(jnp.where(first, sq, 0.0), axis=1, keepdims=True)
        ss_b = jnp.sum(jnp.where(first, 0.0, sq), axis=1, keepdims=True)
        ms = jnp.where(first, ss_a, ss_b) * (1.0 / ATTN_V_DIM)
        o = o * lax.rsqrt(ms + EPS) * g_ref[...] * (1.0 - lambda_init)
        o_ref[pl.ds(pl.multiple_of(row * tq, tq), tq), :] = o.astype(o_ref.dtype)

    bufs = ((s0_sc, mx0_sc), (s1_sc, mx1_sc))

    def step(parity, cur_tile, nxt_row, nxt_tile, nxt_diag):
        def run(cur):
            scores(nxt_row, nxt_tile, *bufs[1 - cur], nxt_diag)
            softmax_pv(*bufs[cur], cur_tile)
        lax.cond(parity == 0, lambda: run(0), lambda: run(1))
        return 1 - parity

    scores(0, 0, s0_sc, mx0_sc, True)

    def row_body(i, parity):
        m_sc[...] = jnp.full(m_sc.shape, NEG_BIG, F32)
        acc_sc[...] = jnp.zeros(acc_sc.shape, F32)
        parity = lax.fori_loop(0, jnp.maximum(i - 1, 0), lambda j, par: step(par, j, i, j + 1, False), parity)
        parity = lax.cond(i > 0, lambda par: step(par, i - 1, i, i, True), lambda par: par, parity)
        parity = step(parity, i, jnp.minimum(i + 1, nq - 1), 0, False)
        finalize(i)
        return parity

    lax.fori_loop(0, nq, row_body, 0)


def _attention(lam, g2, q, k, v, batch, seq, lambda_init):
    t = q.shape[0]
    tq = ATTN_TILE
    n_pairs = ATTN_WIDTH // LANES
    seq_block = pl.BlockSpec((seq, LANES), lambda b, p: (b, p))
    return pl.pallas_call(
        functools.partial(_attn_body, tq=tq, lambda_init=lambda_init),
        out_shape=jax.ShapeDtypeStruct((t, ATTN_WIDTH), BF16),
        grid=(batch, n_pairs),
        in_specs=[_const_spec(lam.shape), _const_spec((1, LANES)), seq_block, seq_block, seq_block],
        out_specs=seq_block,
        scratch_shapes=[
            pltpu.VMEM((N_MAPS, seq, LANES), BF16),
            pltpu.VMEM((N_MAPS, tq, tq), F32),
            pltpu.VMEM((N_MAPS, tq, tq), F32),
            pltpu.VMEM((N_MAPS, tq, LANES), F32),
            pltpu.VMEM((N_MAPS, tq, LANES), F32),
            pltpu.VMEM((N_MAPS, tq, LANES), F32),
            pltpu.VMEM((N_MAPS, tq, LANES), F32),
        ],
        compiler_params=_params("parallel", "parallel"),
        name="attn",
    )(lam, g2, q, k, v)


def _expand_heads(per_head):
    rows = per_head.shape[0]
    first = lax.broadcasted_iota(jnp.int32, (1, LANES), 1) < SSM_HEAD_DIM
    tiles = []
    for t in range(SSM_HEADS // 2):
        a = jnp.broadcast_to(per_head[:, 2 * t:2 * t + 1], (rows, LANES))
        b = jnp.broadcast_to(per_head[:, 2 * t + 1:2 * t + 2], (rows, LANES))
        tiles.append(jnp.where(first, a, b))
    return tiles


def _ssd_body(z_ref, xbc_ref, dt_ref, convw_ref, convb_ref, dtb_ref, alog_ref, dskip_ref, ng_ref,
              o_ref, ubuf, st_sc):
    ch = z_ref.shape[0]
    chunk = pl.program_id(1)

    @pl.when(chunk == 0)
    def _():
        ubuf[0:CONV_HALO, :] = jnp.zeros((CONV_HALO, XBC_COLS), F32)
        st_sc[...] = jnp.zeros(st_sc.shape, F32)

    ubuf[CONV_HALO:CONV_HALO + ch, :] = xbc_ref[...]
    conv = convb_ref[...]
    for tap in range(CONV_WIDTH):
        shift = CONV_WIDTH - 1 - tap
        conv = conv + convw_ref[tap:tap + 1, :] * ubuf[CONV_HALO - shift:CONV_HALO - shift + ch, :]
    ubuf[0:CONV_HALO, :] = ubuf[ch:ch + CONV_HALO, :]
    xa = _silu(conv)

    n_tiles = SSM_WIDTH // LANES
    tiles_per_group = n_tiles // SSM_GROUPS
    xs = [xa[:, t * LANES:(t + 1) * LANES] for t in range(n_tiles)]
    bm = [xa[:, SSM_WIDTH + g * SSM_STATE:SSM_WIDTH + (g + 1) * SSM_STATE].astype(BF16) for g in range(SSM_GROUPS)]
    c_off = SSM_WIDTH + SSM_GROUPS * SSM_STATE
    cm = [xa[:, c_off + g * SSM_STATE:c_off + (g + 1) * SSM_STATE].astype(BF16) for g in range(SSM_GROUPS)]

    dt_in = dt_ref[...] + dtb_ref[...]
    dt = jnp.maximum(dt_in, 0.0) + jnp.log1p(jnp.exp(-jnp.abs(dt_in)))
    da = dt * (-jnp.exp(alog_ref[...]))
    row = lax.broadcasted_iota(jnp.int32, (ch, ch), 0)
    colm = lax.broadcasted_iota(jnp.int32, (ch, ch), 1)
    tri = row >= colm
    acs = jnp.dot(tri.astype(F32), da, preferred_element_type=F32, precision=lax.Precision.HIGHEST)
    acs_t = acs.T
    acs_last = acs[ch - 1:ch, :]

    dt_x = _expand_heads(dt)
    decay_in = _expand_heads(jnp.exp(acs))
    decay_out = _expand_heads(jnp.exp(acs_last - acs))
    first = lax.broadcasted_iota(jnp.int32, (1, LANES), 1) < SSM_HEAD_DIM

    xdt = [xs[t] * dt_x[t] for t in range(n_tiles)]
    y_tiles = []
    for g in range(SSM_GROUPS):
        cb = lax.dot_general(cm[g], bm[g], (((1,), (1,)), ((), ())), preferred_element_type=F32)
        st_prev = st_sc[g]
        y_off = jnp.dot(cm[g], st_prev.astype(BF16), preferred_element_type=F32)
        xdec = []
        for tg in range(tiles_per_group):
            t = g * tiles_per_group + tg
            xdt_b = xdt[t].astype(BF16)
            halves = []
            for h in (2 * t, 2 * t + 1):
                seg = acs[:, h:h + 1] - acs_t[h:h + 1, :]
                gmat = (cb * jnp.exp(jnp.where(tri, seg, -jnp.inf))).astype(BF16)
                halves.append(jnp.dot(gmat, xdt_b, preferred_element_type=F32))
            y_diag = jnp.where(first, halves[0], halves[1])
            y_tiles.append(y_diag + y_off[:, tg * LANES:(tg + 1) * LANES] * decay_in[t]
                           + xs[t] * dskip_ref[:, t * LANES:(t + 1) * LANES])
            xdec.append((xdt[t] * decay_out[t]).astype(BF16))
        xdec_g = jnp.concatenate(xdec, axis=1)
        contrib = lax.dot_general(bm[g], xdec_g, (((0,), (0,)), ((), ())), preferred_element_type=F32)
        chunk_decay = jnp.concatenate(
            [decay_in[g * tiles_per_group + tg][ch - 1:ch, :] for tg in range(tiles_per_group)], axis=1)
        st_sc[g] = st_prev * chunk_decay + contrib

    for g in range(SSM_GROUPS):
        yg = jnp.concatenate(y_tiles[g * tiles_per_group:(g + 1) * tiles_per_group], axis=1)
        cols = slice(g * GROUP_WIDTH, (g + 1) * GROUP_WIDTH)
        yg = yg * _silu(z_ref[:, cols])
        o_ref[:, cols] = _rms(yg, ng_ref[:, cols]).astype(o_ref.dtype)


def _ssd(z, xbc, dt, conv_w, conv_b, dt_bias, a_log, d_skip, norm_g, batch, seq):
    t = z.shape[0]
    ch = SSD_CHUNK
    nc = seq // ch
    tok = lambda b, c: (b * nc + c, 0)
    return pl.pallas_call(
        _ssd_body,
        out_shape=jax.ShapeDtypeStruct((t, SSM_WIDTH), BF16),
        grid=(batch, nc),
        in_specs=[
            pl.BlockSpec((ch, SSM_WIDTH), tok),
            pl.BlockSpec((ch, XBC_COLS), tok),
            pl.BlockSpec((ch, LANES), tok),
            _const_spec((CONV_WIDTH, XBC_COLS)),
            _const_spec((1, XBC_COLS)),
            _const_spec((1, LANES)),
            _const_spec((1, LANES)),
            _const_spec((1, SSM_WIDTH)),
            _const_spec((1, SSM_WIDTH)),
        ],
        out_specs=pl.BlockSpec((ch, SSM_WIDTH), tok),
        scratch_shapes=[
            pltpu.VMEM((CONV_HALO + ch, XBC_COLS), F32),
            pltpu.VMEM((SSM_GROUPS, SSM_STATE, GROUP_WIDTH), F32),
        ],
        compiler_params=_params("parallel", "arbitrary"),
        name="ssd",
    )(z, xbc, dt, conv_w, conv_b, dt_bias, a_log, d_skip, norm_g)


def _outproj_body(x_ref, attn_ref, ssm_ref, wa_ref, ws_ref, g_ref, o_ref):
    mixed = jnp.dot(attn_ref[...], wa_ref[...], preferred_element_type=F32)
    mixed = mixed + jnp.dot(ssm_ref[...], ws_ref[...], preferred_element_type=F32)
    o_ref[...] = x_ref[...] + _rms(mixed, g_ref[...])


def _outproj(x, attn, ssm, wa, ws, g):
    t, d = x.shape
    tm = PROJ_TOKEN_TILE
    return pl.pallas_call(
        _outproj_body,
        out_shape=jax.ShapeDtypeStruct((t, d), F32),
        grid=(t // tm,),
        in_specs=[
            pl.BlockSpec((tm, d), lambda i: (i, 0)),
            pl.BlockSpec((tm, ATTN_WIDTH), lambda i: (i, 0)),
            pl.BlockSpec((tm, SSM_WIDTH), lambda i: (i, 0)),
            _const_spec(wa.shape),
            _const_spec(ws.shape),
            _const_spec((1, d)),
        ],
        out_specs=pl.BlockSpec((tm, d), lambda i: (i, 0)),
        compiler_params=_params("parallel"),
        name="outproj",
    )(x, attn, ssm, wa, ws, g)


def _pad_lanes(row):
    return jnp.pad(row.astype(F32), (0, LANES - row.shape[0]))[None, :]


def _layer(x, p, layer_index, batch, seq):
    lambda_init = 0.8 - 0.6 * math.exp(-0.3 * layer_index)
    row = lambda a: a.astype(F32)[None, :]

    x = _ffn(x, row(p["ffn1_pre_g"]), p["ffn1_w_gate"].astype(BF16), p["ffn1_w_up"].astype(BF16),
             p["ffn1_w_down"].astype(BF16), row(p["ffn1_post_g"]))

    w_in = p["w_in"]
    dt_cols = w_in[:, -SSM_HEADS:]
    w_packed = jnp.concatenate(
        [w_in[:, :-SSM_HEADS], jnp.pad(dt_cols, ((0, 0), (0, LANES - SSM_HEADS)))], axis=1).astype(BF16)
    q, k, v, z, xbc, dt = _inproj(x, row(p["mix_pre_g"]), w_packed)

    f32 = lambda a: a.astype(F32)
    lam = jnp.stack([f32(p[n]) for n in ("lambda_q1", "lambda_k1", "lambda_q2", "lambda_k2")])
    g2 = jnp.tile(row(p["attn_subln_g"]), (1, LANES // ATTN_V_DIM))
    attn = _attention(lam, g2, q, k, v, batch, seq, lambda_init)

    d_skip = jnp.repeat(f32(p["d_skip"]), SSM_HEAD_DIM)[None, :]
    ssm = _ssd(z, xbc, dt, f32(p["conv_w"]), row(p["conv_b"]), _pad_lanes(p["dt_bias"]),
               _pad_lanes(p["a_log"]), d_skip, row(p["ssm_norm_g"]), batch, seq)

    w_out = p["w_out"].astype(BF16)
    x = _outproj(x, attn, ssm, w_out[:ATTN_WIDTH], w_out[ATTN_WIDTH:], row(p["mix_post_g"]))

    return _ffn(x, row(p["ffn2_pre_g"]), p["ffn2_w_gate"].astype(BF16), p["ffn2_w_up"].astype(BF16),
                p["ffn2_w_down"].astype(BF16), row(p["ffn2_post_g"]))


def kernel(x, ffn1_pre_g, ffn1_w_gate, ffn1_w_up, ffn1_w_down, ffn1_post_g, mix_pre_g, w_in, lambda_q1, lambda_k1, lambda_q2, lambda_k2, attn_subln_g, conv_w, conv_b, dt_bias, a_log, d_skip, ssm_norm_g, w_out, mix_post_g, ffn2_pre_g, ffn2_w_gate, ffn2_w_up, ffn2_w_down, ffn2_post_g):
    params = dict(
        ffn1_pre_g=ffn1_pre_g, ffn1_w_gate=ffn1_w_gate, ffn1_w_up=ffn1_w_up, ffn1_w_down=ffn1_w_down,
        ffn1_post_g=ffn1_post_g, mix_pre_g=mix_pre_g, w_in=w_in, lambda_q1=lambda_q1, lambda_k1=lambda_k1,
        lambda_q2=lambda_q2, lambda_k2=lambda_k2, attn_subln_g=attn_subln_g, conv_w=conv_w, conv_b=conv_b,
        dt_bias=dt_bias, a_log=a_log, d_skip=d_skip, ssm_norm_g=ssm_norm_g, w_out=w_out,
        mix_post_g=mix_post_g, ffn2_pre_g=ffn2_pre_g, ffn2_w_gate=ffn2_w_gate, ffn2_w_up=ffn2_w_up,
        ffn2_w_down=ffn2_w_down, ffn2_post_g=ffn2_post_g)
    batch, seq, d = x.shape
    h = x.reshape(batch * seq, d)
    for i in range(ffn1_pre_g.shape[0]):
        h = _layer(h, {name: a[i] for name, a in params.items()}, i, batch, seq)
    return h.reshape(batch, seq, d)
```

```python
import functools
import math

import jax
import jax.numpy as jnp
from jax import lax
from jax.experimental import pallas as pl
from jax.experimental.pallas import tpu as pltpu

F32 = jnp.float32
BF16 = jnp.bfloat16

EPS = 1e-6
LANES = 128
ATTN_HEADS = 8
ATTN_QK_DIM = 32
ATTN_V_DIM = 64
ATTN_WIDTH = ATTN_HEADS * ATTN_V_DIM
SSM_HEADS = 16
SSM_HEAD_DIM = 64
SSM_WIDTH = SSM_HEADS * SSM_HEAD_DIM
SSM_GROUPS = 2
SSM_STATE = 128
CONV_WIDTH = 4
XBC_COLS = SSM_WIDTH + 2 * SSM_GROUPS * SSM_STATE
GROUP_WIDTH = SSM_WIDTH // SSM_GROUPS
NEG_BIG = -1e30

VMEM_LIMIT_BYTES = 56 * 1024 * 1024

FFN_TOKEN_TILE = 512
FFN_FF_CHUNK = 256
PROJ_TOKEN_TILE = 512
ATTN_TILE = 1024
SSD_CHUNK = 128
CONV_HALO = 8


def _rms(xf, g):
    return xf * lax.rsqrt(jnp.mean(xf * xf, axis=-1, keepdims=True) + EPS) * g


def _silu(v):
    return v / (1.0 + jnp.exp(-v))


def _params(*semantics):
    return pltpu.CompilerParams(dimension_semantics=semantics, vmem_limit_bytes=VMEM_LIMIT_BYTES)


def _const_spec(shape):
    return pl.BlockSpec(shape, lambda *_: (0,) * len(shape))


def _ffn_body(x_ref, pre_g_ref, wg_ref, wu_ref, wd_ref, post_g_ref, o_ref):
    x = x_ref[...]
    h = _rms(x, pre_g_ref[...]).astype(BF16)
    d_ff = wg_ref.shape[1]
    y = jnp.zeros(x.shape, F32)
    for c in range(d_ff // FFN_FF_CHUNK):
        cols = slice(c * FFN_FF_CHUNK, (c + 1) * FFN_FF_CHUNK)
        g = jnp.dot(h, wg_ref[:, cols], preferred_element_type=F32)
        u = jnp.dot(h, wu_ref[:, cols], preferred_element_type=F32)
        act = (_silu(g) * u).astype(BF16)
        y = y + jnp.dot(act, wd_ref[cols, :], preferred_element_type=F32)
    o_ref[...] = x + 0.5 * _rms(y, post_g_ref[...])


def _ffn(x, pre_g, wg, wu, wd, post_g):
    t, d = x.shape
    d_ff = wg.shape[1]
    tm = FFN_TOKEN_TILE
    return pl.pallas_call(
        _ffn_body,
        out_shape=jax.ShapeDtypeStruct((t, d), F32),
        grid=(t // tm,),
        in_specs=[
            pl.BlockSpec((tm, d), lambda i: (i, 0)),
            _const_spec((1, d)),
            _const_spec((d, d_ff)),
            _const_spec((d, d_ff)),
            _const_spec((d_ff, d)),
            _const_spec((1, d)),
        ],
        out_specs=pl.BlockSpec((tm, d), lambda i: (i, 0)),
        compiler_params=_params("parallel"),
        name="ffn",
    )(x, pre_g, wg, wu, wd, post_g)


def _inproj_body(x_ref, g_ref, w_ref, convw_ref, convb_ref, q_ref, k_ref, v_ref, gz_ref, xa_ref, dt_ref,
                 ubuf, *, tiles_per_seq):
    tm = x_ref.shape[0]

    @pl.when(pl.program_id(0) % tiles_per_seq == 0)
    def _():
        ubuf[0:CONV_HALO, :] = jnp.zeros((CONV_HALO, XBC_COLS), F32)

    h = _rms(x_ref[...], g_ref[...]).astype(BF16)
    offsets = {}
    col = 0
    for name, width in (("q", ATTN_WIDTH), ("k", ATTN_WIDTH), ("v", ATTN_WIDTH), ("z", SSM_WIDTH),
                        ("xbc", XBC_COLS), ("dt", LANES)):
        offsets[name] = (col, col + width)
        col += width

    def proj(name):
        lo, hi = offsets[name]
        return jnp.dot(h, w_ref[:, lo:hi], preferred_element_type=F32)

    ubuf[CONV_HALO:CONV_HALO + tm, :] = proj("xbc")
    conv = convb_ref[...]
    for tap in range(CONV_WIDTH):
        shift = CONV_WIDTH - 1 - tap
        conv = conv + convw_ref[tap:tap + 1, :] * ubuf[CONV_HALO - shift:CONV_HALO - shift + tm, :]
    ubuf[0:CONV_HALO, :] = ubuf[tm:tm + CONV_HALO, :]
    xa_ref[...] = _silu(conv)

    q_ref[...] = (proj("q") * (ATTN_QK_DIM ** -0.5 * math.log2(math.e))).astype(BF16)
    k_ref[...] = proj("k").astype(BF16)
    v_ref[...] = proj("v").astype(BF16)
    gz_ref[...] = _silu(proj("z"))
    dt_ref[...] = proj("dt")


def _inproj(x, g, w, conv_w, conv_b, seq):
    t, d = x.shape
    tm = PROJ_TOKEN_TILE
    widths = (ATTN_WIDTH, ATTN_WIDTH, ATTN_WIDTH, SSM_WIDTH, XBC_COLS, LANES)
    dtypes = (BF16, BF16, BF16, F32, F32, F32)
    return pl.pallas_call(
        functools.partial(_inproj_body, tiles_per_seq=seq // tm),
        out_shape=[jax.ShapeDtypeStruct((t, n), dt) for n, dt in zip(widths, dtypes)],
        grid=(t // tm,),
        in_specs=[
            pl.BlockSpec((tm, d), lambda i: (i, 0)),
            _const_spec((1, d)),
            _const_spec(w.shape),
            _const_spec((CONV_WIDTH, XBC_COLS)),
            _const_spec((1, XBC_COLS)),
        ],
        out_specs=[pl.BlockSpec((tm, n), lambda i: (i, 0)) for n in widths],
        scratch_shapes=[pltpu.VMEM((CONV_HALO + tm, XBC_COLS), F32)],
        compiler_params=_params("arbitrary"),
        name="inproj",
    )(x, g, w, conv_w, conv_b)


N_MAPS = 4


def _attn_body(lam_ref, g_ref, q_ref, k_ref, v_ref, o_ref, s0_sc, s1_sc, mx0_sc, mx1_sc, m_sc, acc_sc, *,
               tq, lambda_init):
    seq = q_ref.shape[0]
    tk = tq // 2
    nq = seq // tq
    n_lane_tiles = tk // LANES
    pair = pl.program_id(1)
    log2e = math.log2(math.e)

    lane = lax.broadcasted_iota(jnp.int32, (1, LANES), 1)
    first = lane < ATTN_V_DIM
    head_a = (2 * pair + 1).astype(F32)
    slope = [jnp.exp2(-(head_a + (m // 2))) * log2e for m in range(N_MAPS)]
    col_pos = lax.broadcasted_iota(jnp.int32, (1, LANES), 1).astype(F32)

    def scores(row, tile, s_ref, mx_ref, row_lo):
        q = q_ref[pl.ds(pl.multiple_of(row * tq + row_lo, tk), tq - row_lo), :]
        kt = k_ref[pl.ds(pl.multiple_of(tile * tk, tk), tk), :]
        rel = lax.convert_element_type(tile * tk - row * tq, F32)
        for m in range(N_MAPS):
            qm = jnp.where((lane >= 32 * m) & (lane < 32 * (m + 1)), q, jnp.zeros_like(q))
            s = lax.dot_general(qm, kt, (((1,), (1,)), ((), ())), preferred_element_type=F32)
            tile_max = None
            for t in range(n_lane_tiles):
                st = s[:, t * LANES:(t + 1) * LANES] + slope[m] * (col_pos + (rel + t * LANES))
                s_ref[m, row_lo:tq, t * LANES:(t + 1) * LANES] = st
                tile_max = st if tile_max is None else jnp.maximum(tile_max, st)
            mx_ref[m, row_lo:tq, :] = tile_max

    def softmax_pv(s_ref, mx_ref, tile, diag):
        row_lo = tk if diag == 1 else 0
        rs = slice(row_lo, tq)
        vt = v_ref[pl.ds(pl.multiple_of(tile * tk, tk), tk), :]
        ones = jnp.ones_like(vt)
        v_heads = (jnp.where(first, vt, ones), jnp.where(first, ones, vt))

        def load(m, t):
            st = s_ref[m, rs, t * LANES:(t + 1) * LANES]
            if diag is None:
                return st
            shape = (tq - row_lo, LANES)
            visible = (lax.broadcasted_iota(jnp.int32, shape, 0) + row_lo
                       >= lax.broadcasted_iota(jnp.int32, shape, 1) + (diag * tk + t * LANES))
            return jnp.where(visible, st, -jnp.inf)

        for m in range(N_MAPS):
            if diag is None:
                tile_max = mx_ref[m, rs, :]
            else:
                tile_max = load(m, 0)
                for t in range(1, n_lane_tiles):
                    tile_max = jnp.maximum(tile_max, load(m, t))
            m_old = m_sc[m, rs, :]
            m_new = jnp.maximum(m_old, jnp.max(tile_max, axis=1, keepdims=True))
            m_sc[m, rs, :] = m_new
            acc_sc[m, rs, :] = acc_sc[m, rs, :] * jnp.exp2(m_old - m_new)
        for m in range(N_MAPS):
            p = jnp.concatenate(
                [jnp.exp2(load(m, t) - m_sc[m, rs, :]).astype(BF16) for t in range(n_lane_tiles)], axis=1)
            acc_sc[m, rs, :] += jnp.dot(p, v_heads[m // 2], preferred_element_type=F32)

    def finalize(row):
        lp = lam_ref[...]
        lam = (jnp.exp(jnp.sum(lp[0:1] * lp[1:2], axis=1, keepdims=True))
               - jnp.exp(jnp.sum(lp[2:3] * lp[3:4], axis=1, keepdims=True)) + lambda_init)
        ratio = [acc_sc[m] / pltpu.roll(acc_sc[m], ATTN_V_DIM, 1) for m in range(N_MAPS)]
        o = jnp.where(first, ratio[0] - lam * ratio[1], ratio[2] - lam * ratio[3])
        sq = o * o
        ss_a = jnp.sum(jnp.where(first, sq, 0.0), axis=1, keepdims=True)
        ss_b = jnp.sum(jnp.where(first, 0.0, sq), axis=1, keepdims=True)
        ms = jnp.where(first, ss_a, ss_b) * (1.0 / ATTN_V_DIM)
        o = o * lax.rsqrt(ms + EPS) * g_ref[...] * (1.0 - lambda_init)
        o_ref[pl.ds(pl.multiple_of(row * tq, tq), tq), :] = o.astype(o_ref.dtype)

    bufs = ((s0_sc, mx0_sc), (s1_sc, mx1_sc))

    def step(parity, cur_tile, cur_diag, nxt_row, nxt_tile, nxt_row_lo):
        def run(cur):
            scores(nxt_row, nxt_tile, *bufs[1 - cur], nxt_row_lo)
            softmax_pv(*bufs[cur], cur_tile, cur_diag)
        lax.cond(parity == 0, lambda: run(0), lambda: run(1))
        return 1 - parity

    scores(0, 0, s0_sc, mx0_sc, 0)

    def row_body(i, parity):
        m_sc[...] = jnp.full(m_sc.shape, NEG_BIG, F32)
        acc_sc[...] = jnp.zeros(acc_sc.shape, F32)
        parity = lax.fori_loop(0, 2 * i, lambda j, par: step(par, j, None, i, j + 1, 0), parity)
        parity = step(parity, 2 * i, 0, i, 2 * i + 1, tk)
        parity = step(parity, 2 * i + 1, 1, jnp.minimum(i + 1, nq - 1), 0, 0)
        finalize(i)
        return parity

    lax.fori_loop(0, nq, row_body, 0)


def _attention(lam, g2, q, k, v, batch, seq, lambda_init):
    t = q.shape[0]
    tq = ATTN_TILE
    n_pairs = ATTN_WIDTH // LANES
    seq_block = pl.BlockSpec((seq, LANES), lambda b, p: (b, p))
    return pl.pallas_call(
        functools.partial(_attn_body, tq=tq, lambda_init=lambda_init),
        out_shape=jax.ShapeDtypeStruct((t, ATTN_WIDTH), BF16),
        grid=(batch, n_pairs),
        in_specs=[_const_spec(lam.shape), _const_spec((1, LANES)), seq_block, seq_block, seq_block],
        out_specs=seq_block,
        scratch_shapes=[
            pltpu.VMEM((N_MAPS, tq, tq // 2), F32),
            pltpu.VMEM((N_MAPS, tq, tq // 2), F32),
            pltpu.VMEM((N_MAPS, tq, LANES), F32),
            pltpu.VMEM((N_MAPS, tq, LANES), F32),
            pltpu.VMEM((N_MAPS, tq, LANES), F32),
            pltpu.VMEM((N_MAPS, tq, LANES), F32),
        ],
        compiler_params=_params("parallel", "parallel"),
        name="attn",
    )(lam, g2, q, k, v)


def _lane_bcast(per_head, h):
    return jnp.broadcast_to(per_head[:, h:h + 1], (per_head.shape[0], LANES))


def _ssd_body(gz_ref, xa_ref, dt_ref, dtb_ref, alog_ref, dskip_ref, ng_ref, o_ref, st_sc):
    ch = gz_ref.shape[0]
    chunk = pl.program_id(1)

    @pl.when(chunk == 0)
    def _():
        st_sc[...] = jnp.zeros(st_sc.shape, F32)

    xa = xa_ref[...]
    n_tiles = SSM_WIDTH // LANES
    tiles_per_group = n_tiles // SSM_GROUPS
    xs = [xa[:, t * LANES:(t + 1) * LANES] for t in range(n_tiles)]
    bm = [xa[:, SSM_WIDTH + g * SSM_STATE:SSM_WIDTH + (g + 1) * SSM_STATE].astype(BF16) for g in range(SSM_GROUPS)]
    c_off = SSM_WIDTH + SSM_GROUPS * SSM_STATE
    cm = [xa[:, c_off + g * SSM_STATE:c_off + (g + 1) * SSM_STATE].astype(BF16) for g in range(SSM_GROUPS)]

    dt_in = dt_ref[...] + dtb_ref[...]
    dt = jnp.maximum(dt_in, 0.0) + jnp.log1p(jnp.exp(-jnp.abs(dt_in)))
    da = dt * (-jnp.exp(alog_ref[...]))
    row = lax.broadcasted_iota(jnp.int32, (ch, ch), 0)
    colm = lax.broadcasted_iota(jnp.int32, (ch, ch), 1)
    tri = row >= colm
    acs = jnp.dot(tri.astype(F32), da, preferred_element_type=F32, precision=lax.Precision.HIGHEST)
    acs_t = acs.T
    first = lax.broadcasted_iota(jnp.int32, (1, LANES), 1) < SSM_HEAD_DIM

    y_tiles = []
    for g in range(SSM_GROUPS):
        cb = lax.dot_general(cm[g], bm[g], (((1,), (1,)), ((), ())), preferred_element_type=F32)
        st_prev = st_sc[g]
        y_off = jnp.dot(cm[g], st_prev.astype(BF16), preferred_element_type=F32)
        xdec, chunk_decay = [], []
        for tg in range(tiles_per_group):
            t = g * tiles_per_group + tg
            heads = (2 * t, 2 * t + 1)
            acs_b = [_lane_bcast(acs, h) for h in heads]
            acs_pair = jnp.where(first, acs_b[0], acs_b[1])
            dt_pair = jnp.where(first, _lane_bcast(dt, heads[0]), _lane_bcast(dt, heads[1]))
            decay_in = jnp.exp(acs_pair)
            decay_out = jnp.exp(acs_pair[ch - 1:ch, :] - acs_pair)
            xdt = xs[t] * dt_pair
            xdt_b = xdt.astype(BF16)
            halves = []
            for i, h in enumerate(heads):
                seg = acs_b[i] - acs_t[h:h + 1, :]
                gmat = (cb * jnp.exp(jnp.where(tri, seg, -jnp.inf))).astype(BF16)
                halves.append(jnp.dot(gmat, xdt_b, preferred_element_type=F32))
            y_diag = jnp.where(first, halves[0], halves[1])
            y_tiles.append(y_diag + y_off[:, tg * LANES:(tg + 1) * LANES] * decay_in
                           + xs[t] * dskip_ref[:, t * LANES:(t + 1) * LANES])
            xdec.append((xdt * decay_out).astype(BF16))
            chunk_decay.append(decay_in[ch - 1:ch, :])
        xdec_g = jnp.concatenate(xdec, axis=1)
        contrib = lax.dot_general(bm[g], xdec_g, (((0,), (0,)), ((), ())), preferred_element_type=F32)
        st_sc[g] = st_prev * jnp.concatenate(chunk_decay, axis=1) + contrib

    for g in range(SSM_GROUPS):
        yg = jnp.concatenate(y_tiles[g * tiles_per_group:(g + 1) * tiles_per_group], axis=1)
        cols = slice(g * GROUP_WIDTH, (g + 1) * GROUP_WIDTH)
        yg = yg * gz_ref[:, cols]
        o_ref[:, cols] = _rms(yg, ng_ref[:, cols]).astype(o_ref.dtype)


def _ssd(gz, xa, dt, dt_bias, a_log, d_skip, norm_g, batch, seq):
    t = gz.shape[0]
    ch = SSD_CHUNK
    nc = seq // ch
    tok = lambda b, c: (b * nc + c, 0)
    return pl.pallas_call(
        _ssd_body,
        out_shape=jax.ShapeDtypeStruct((t, SSM_WIDTH), BF16),
        grid=(batch, nc),
        in_specs=[
            pl.BlockSpec((ch, SSM_WIDTH), tok),
            pl.BlockSpec((ch, XBC_COLS), tok),
            pl.BlockSpec((ch, LANES), tok),
            _const_spec((1, LANES)),
            _const_spec((1, LANES)),
            _const_spec((1, SSM_WIDTH)),
            _const_spec((1, SSM_WIDTH)),
        ],
        out_specs=pl.BlockSpec((ch, SSM_WIDTH), tok),
        scratch_shapes=[pltpu.VMEM((SSM_GROUPS, SSM_STATE, GROUP_WIDTH), F32)],
        compiler_params=_params("parallel", "arbitrary"),
        name="ssd",
    )(gz, xa, dt, dt_bias, a_log, d_skip, norm_g)


def _outproj_body(x_ref, attn_ref, ssm_ref, wa_ref, ws_ref, g_ref, o_ref):
    mixed = jnp.dot(attn_ref[...], wa_ref[...], preferred_element_type=F32)
    mixed = mixed + jnp.dot(ssm_ref[...], ws_ref[...], preferred_element_type=F32)
    o_ref[...] = x_ref[...] + _rms(mixed, g_ref[...])


def _outproj(x, attn, ssm, wa, ws, g):
    t, d = x.shape
    tm = PROJ_TOKEN_TILE
    return pl.pallas_call(
        _outproj_body,
        out_shape=jax.ShapeDtypeStruct((t, d), F32),
        grid=(t // tm,),
        in_specs=[
            pl.BlockSpec((tm, d), lambda i: (i, 0)),
            pl.BlockSpec((tm, ATTN_WIDTH), lambda i: (i, 0)),
            pl.BlockSpec((tm, SSM_WIDTH), lambda i: (i, 0)),
            _const_spec(wa.shape),
            _const_spec(ws.shape),
            _const_spec((1, d)),
        ],
        out_specs=pl.BlockSpec((tm, d), lambda i: (i, 0)),
        compiler_params=_params("parallel"),
        name="outproj",
    )(x, attn, ssm, wa, ws, g)


def _pad_lanes(row):
    return jnp.pad(row.astype(F32), (0, LANES - row.shape[0]))[None, :]


def _layer(x, p, layer_index, batch, seq):
    lambda_init = 0.8 - 0.6 * math.exp(-0.3 * layer_index)
    row = lambda a: a.astype(F32)[None, :]

    x = _ffn(x, row(p["ffn1_pre_g"]), p["ffn1_w_gate"].astype(BF16), p["ffn1_w_up"].astype(BF16),
             p["ffn1_w_down"].astype(BF16), row(p["ffn1_post_g"]))

    w_in = p["w_in"]
    dt_cols = w_in[:, -SSM_HEADS:]
    w_packed = jnp.concatenate(
        [w_in[:, :-SSM_HEADS], jnp.pad(dt_cols, ((0, 0), (0, LANES - SSM_HEADS)))], axis=1).astype(BF16)
    f32 = lambda a: a.astype(F32)
    q, k, v, gz, xa, dt = _inproj(x, row(p["mix_pre_g"]), w_packed, f32(p["conv_w"]), row(p["conv_b"]), seq)

    lam = jnp.stack([f32(p[n]) for n in ("lambda_q1", "lambda_k1", "lambda_q2", "lambda_k2")])
    g2 = jnp.tile(row(p["attn_subln_g"]), (1, LANES // ATTN_V_DIM))
    attn = _attention(lam, g2, q, k, v, batch, seq, lambda_init)

    d_skip = jnp.repeat(f32(p["d_skip"]), SSM_HEAD_DIM)[None, :]
    ssm = _ssd(gz, xa, dt, _pad_lanes(p["dt_bias"]), _pad_lanes(p["a_log"]), d_skip,
               row(p["ssm_norm_g"]), batch, seq)

    w_out = p["w_out"].astype(BF16)
    x = _outproj(x, attn, ssm, w_out[:ATTN_WIDTH], w_out[ATTN_WIDTH:], row(p["mix_post_g"]))

    return _ffn(x, row(p["ffn2_pre_g"]), p["ffn2_w_gate"].astype(BF16), p["ffn2_w_up"].astype(BF16),
                p["ffn2_w_down"].astype(BF16), row(p["ffn2_post_g"]))


def kernel(x, ffn1_pre_g, ffn1_w_gate, ffn1_w_up, ffn1_w_down, ffn1_post_g, mix_pre_g, w_in, lambda_q1, lambda_k1, lambda_q2, lambda_k2, attn_subln_g, conv_w, conv_b, dt_bias, a_log, d_skip, ssm_norm_g, w_out, mix_post_g, ffn2_pre_g, ffn2_w_gate, ffn2_w_up, ffn2_w_down, ffn2_post_g):
    params = dict(
        ffn1_pre_g=ffn1_pre_g, ffn1_w_gate=ffn1_w_gate, ffn1_w_up=ffn1_w_up, ffn1_w_down=ffn1_w_down,
        ffn1_post_g=ffn1_post_g, mix_pre_g=mix_pre_g, w_in=w_in, lambda_q1=lambda_q1, lambda_k1=lambda_k1,
        lambda_q2=lambda_q2, lambda_k2=lambda_k2, attn_subln_g=attn_subln_g, conv_w=conv_w, conv_b=conv_b,
        dt_bias=dt_bias, a_log=a_log, d_skip=d_skip, ssm_norm_g=ssm_norm_g, w_out=w_out,
        mix_post_g=mix_post_g, ffn2_pre_g=ffn2_pre_g, ffn2_w_gate=ffn2_w_gate, ffn2_w_up=ffn2_w_up,
        ffn2_w_down=ffn2_w_down, ffn2_post_g=ffn2_post_g)
    batch, seq, d = x.shape
    h = x.reshape(batch * seq, d)
    for i in range(ffn1_pre_g.shape[0]):
        h = _layer(h, {name: a[i] for name, a in params.items()}, i, batch, seq)
    return h.reshape(batch, seq, d)
```

```python
import functools
import math

import jax
import jax.numpy as jnp
from jax import lax
from jax.experimental import pallas as pl
from jax.experimental.pallas import tpu as pltpu

F32 = jnp.float32
BF16 = jnp.bfloat16

EPS = 1e-6
LANES = 128
ATTN_HEADS = 8
ATTN_QK_DIM = 32
ATTN_V_DIM = 64
ATTN_WIDTH = ATTN_HEADS * ATTN_V_DIM
SSM_HEADS = 16
SSM_HEAD_DIM = 64
SSM_WIDTH = SSM_HEADS * SSM_HEAD_DIM
SSM_GROUPS = 2
SSM_STATE = 128
CONV_WIDTH = 4
XBC_COLS = SSM_WIDTH + 2 * SSM_GROUPS * SSM_STATE
GROUP_WIDTH = SSM_WIDTH // SSM_GROUPS
NEG_BIG = -1e30

VMEM_LIMIT_BYTES = 56 * 1024 * 1024

FFN_TOKEN_TILE = 512
FFN_FF_CHUNK = 256
PROJ_TOKEN_TILE = 512
ATTN_TILE = 1024
SSD_CHUNK = 128
CONV_HALO = 8


def _rms(xf, g):
    return xf * lax.rsqrt(jnp.mean(xf * xf, axis=-1, keepdims=True) + EPS) * g


def _silu(v):
    return v / (1.0 + jnp.exp(-v))


def _params(*semantics):
    return pltpu.CompilerParams(dimension_semantics=semantics, vmem_limit_bytes=VMEM_LIMIT_BYTES)


def _const_spec(shape):
    return pl.BlockSpec(shape, lambda *_: (0,) * len(shape))


def _ffn_body(x_ref, pre_g_ref, wg_ref, wu_ref, wd_ref, post_g_ref, o_ref):
    x = x_ref[...]
    h = _rms(x, pre_g_ref[...]).astype(BF16)
    d_ff = wg_ref.shape[1]
    y = jnp.zeros(x.shape, F32)
    for c in range(d_ff // FFN_FF_CHUNK):
        cols = slice(c * FFN_FF_CHUNK, (c + 1) * FFN_FF_CHUNK)
        g = jnp.dot(h, wg_ref[:, cols], preferred_element_type=F32)
        u = jnp.dot(h, wu_ref[:, cols], preferred_element_type=F32)
        act = (_silu(g) * u).astype(BF16)
        y = y + jnp.dot(act, wd_ref[cols, :], preferred_element_type=F32)
    o_ref[...] = x + 0.5 * _rms(y, post_g_ref[...])


def _ffn(x, pre_g, wg, wu, wd, post_g):
    t, d = x.shape
    d_ff = wg.shape[1]
    tm = FFN_TOKEN_TILE
    return pl.pallas_call(
        _ffn_body,
        out_shape=jax.ShapeDtypeStruct((t, d), F32),
        grid=(t // tm,),
        in_specs=[
            pl.BlockSpec((tm, d), lambda i: (i, 0)),
            _const_spec((1, d)),
            _const_spec((d, d_ff)),
            _const_spec((d, d_ff)),
            _const_spec((d_ff, d)),
            _const_spec((1, d)),
        ],
        out_specs=pl.BlockSpec((tm, d), lambda i: (i, 0)),
        compiler_params=_params("parallel"),
        name="ffn",
    )(x, pre_g, wg, wu, wd, post_g)


def _inproj_body(x_ref, g_ref, w_ref, convw_ref, convb_ref, q_ref, k_ref, v_ref, gz_ref, xa_ref, dt_ref,
                 ubuf, *, tiles_per_seq):
    tm = x_ref.shape[0]

    @pl.when(pl.program_id(0) % tiles_per_seq == 0)
    def _():
        ubuf[0:CONV_HALO, :] = jnp.zeros((CONV_HALO, XBC_COLS), F32)

    h = _rms(x_ref[...], g_ref[...]).astype(BF16)
    offsets = {}
    col = 0
    for name, width in (("q", ATTN_WIDTH), ("k", ATTN_WIDTH), ("v", ATTN_WIDTH), ("z", SSM_WIDTH),
                        ("xbc", XBC_COLS), ("dt", LANES)):
        offsets[name] = (col, col + width)
        col += width

    def proj(name):
        lo, hi = offsets[name]
        return jnp.dot(h, w_ref[:, lo:hi], preferred_element_type=F32)

    ubuf[CONV_HALO:CONV_HALO + tm, :] = proj("xbc")
    conv = convb_ref[...]
    for tap in range(CONV_WIDTH):
        shift = CONV_WIDTH - 1 - tap
        conv = conv + convw_ref[tap:tap + 1, :] * ubuf[CONV_HALO - shift:CONV_HALO - shift + tm, :]
    ubuf[0:CONV_HALO, :] = ubuf[tm:tm + CONV_HALO, :]
    xa_ref[...] = _silu(conv)

    q_ref[...] = (proj("q") * (ATTN_QK_DIM ** -0.5 * math.log2(math.e))).astype(BF16)
    k_ref[...] = proj("k").astype(BF16)
    v_ref[...] = proj("v").astype(BF16)
    gz_ref[...] = _silu(proj("z"))
    dt_ref[...] = proj("dt")


def _inproj(x, g, w, conv_w, conv_b, seq):
    t, d = x.shape
    tm = PROJ_TOKEN_TILE
    widths = (ATTN_WIDTH, ATTN_WIDTH, ATTN_WIDTH, SSM_WIDTH, XBC_COLS, LANES)
    dtypes = (BF16, BF16, BF16, F32, F32, F32)
    return pl.pallas_call(
        functools.partial(_inproj_body, tiles_per_seq=seq // tm),
        out_shape=[jax.ShapeDtypeStruct((t, n), dt) for n, dt in zip(widths, dtypes)],
        grid=(t // tm,),
        in_specs=[
            pl.BlockSpec((tm, d), lambda i: (i, 0)),
            _const_spec((1, d)),
            _const_spec(w.shape),
            _const_spec((CONV_WIDTH, XBC_COLS)),
            _const_spec((1, XBC_COLS)),
        ],
        out_specs=[pl.BlockSpec((tm, n), lambda i: (i, 0)) for n in widths],
        scratch_shapes=[pltpu.VMEM((CONV_HALO + tm, XBC_COLS), F32)],
        compiler_params=_params("arbitrary"),
        name="inproj",
    )(x, g, w, conv_w, conv_b)


N_MAPS = 4


def _attn_body(lam_ref, g_ref, q_ref, k_ref, v_ref, o_ref, s0_sc, s1_sc, mx0_sc, mx1_sc, m_sc, acc_sc, *,
               tq, lambda_init):
    seq = q_ref.shape[0]
    tk = tq // 2
    nq = seq // tq
    n_lane_tiles = tk // LANES
    pair = pl.program_id(1)
    log2e = math.log2(math.e)

    lane = lax.broadcasted_iota(jnp.int32, (1, LANES), 1)
    first = lane < ATTN_V_DIM
    head_a = (2 * pair + 1).astype(F32)
    slope = [jnp.exp2(-(head_a + (m // 2))) * log2e for m in range(N_MAPS)]
    col_pos = lax.broadcasted_iota(jnp.int32, (1, LANES), 1).astype(F32)

    def scores(row, tile, s_ref, mx_ref, row_lo):
        q = q_ref[pl.ds(pl.multiple_of(row * tq + row_lo, tk), tq - row_lo), :]
        kt = k_ref[pl.ds(pl.multiple_of(tile * tk, tk), tk), :]
        rel = lax.convert_element_type(tile * tk - row * tq, F32)
        for m in range(N_MAPS):
            qm = jnp.where((lane >= 32 * m) & (lane < 32 * (m + 1)), q, jnp.zeros_like(q))
            s = lax.dot_general(qm, kt, (((1,), (1,)), ((), ())), preferred_element_type=F32)
            tile_max = None
            for t in range(n_lane_tiles):
                st = s[:, t * LANES:(t + 1) * LANES] + slope[m] * (col_pos + (rel + t * LANES))
                s_ref[m, row_lo:tq, t * LANES:(t + 1) * LANES] = st
                tile_max = st if tile_max is None else jnp.maximum(tile_max, st)
            mx_ref[m, row_lo:tq, :] = tile_max

    def softmax_pv(s_ref, mx_ref, tile, diag):
        row_lo = tk if diag == 1 else 0
        rs = slice(row_lo, tq)
        vt = v_ref[pl.ds(pl.multiple_of(tile * tk, tk), tk), :]
        ones = jnp.ones_like(vt)
        v_heads = (jnp.where(first, vt, ones), jnp.where(first, ones, vt))

        def load(m, t):
            st = s_ref[m, rs, t * LANES:(t + 1) * LANES]
            if diag is None:
                return st
            shape = (tq - row_lo, LANES)
            visible = (lax.broadcasted_iota(jnp.int32, shape, 0) + row_lo
                       >= lax.broadcasted_iota(jnp.int32, shape, 1) + (diag * tk + t * LANES))
            return jnp.where(visible, st, -jnp.inf)

        for m in range(N_MAPS):
            if diag is None:
                tile_max = mx_ref[m, rs, :]
            else:
                tile_max = load(m, 0)
                for t in range(1, n_lane_tiles):
                    tile_max = jnp.maximum(tile_max, load(m, t))
            m_old = m_sc[m, rs, :]
            m_new = jnp.maximum(m_old, jnp.max(tile_max, axis=1, keepdims=True))
            m_sc[m, rs, :] = m_new
            acc_sc[m, rs, :] = acc_sc[m, rs, :] * jnp.exp2(m_old - m_new)
        for m in range(N_MAPS):
            p = jnp.concatenate(
                [jnp.exp2(load(m, t) - m_sc[m, rs, :]).astype(BF16) for t in range(n_lane_tiles)], axis=1)
            acc_sc[m, rs, :] += jnp.dot(p, v_heads[m // 2], preferred_element_type=F32)

    def finalize(row):
        lp = lam_ref[...]
        lam = (jnp.exp(jnp.sum(lp[0:1] * lp[1:2], axis=1, keepdims=True))
               - jnp.exp(jnp.sum(lp[2:3] * lp[3:4], axis=1, keepdims=True)) + lambda_init)
        ratio = [acc_sc[m] / pltpu.roll(acc_sc[m], ATTN_V_DIM, 1) for m in range(N_MAPS)]
        o = jnp.where(first, ratio[0] - lam * ratio[1], ratio[2] - lam * ratio[3])
        sq = o * o
        ss_a = jnp.sum(jnp.where(first, sq, 0.0), axis=1, keepdims=True)
        ss_b = jnp.sum(jnp.where(first, 0.0, sq), axis=1, keepdims=True)
        ms = jnp.where(first, ss_a, ss_b) * (1.0 / ATTN_V_DIM)
        o = o * lax.rsqrt(ms + EPS) * g_ref[...] * (1.0 - lambda_init)
        o_ref[pl.ds(pl.multiple_of(row * tq, tq), tq), :] = o.astype(o_ref.dtype)

    bufs = ((s0_sc, mx0_sc), (s1_sc, mx1_sc))

    def step(parity, cur_tile, cur_diag, nxt_row, nxt_tile, nxt_row_lo):
        def run(cur):
            scores(nxt_row, nxt_tile, *bufs[1 - cur], nxt_row_lo)
            softmax_pv(*bufs[cur], cur_tile, cur_diag)
        lax.cond(parity == 0, lambda: run(0), lambda: run(1))
        return 1 - parity

    scores(0, 0, s0_sc, mx0_sc, 0)

    def row_body(i, parity):
        m_sc[...] = jnp.full(m_sc.shape, NEG_BIG, F32)
        acc_sc[...] = jnp.zeros(acc_sc.shape, F32)
        def plain_tiles(cur):
            def tile_pair(jj, carry):
                scores(i, 2 * jj + 1, *bufs[1 - cur], 0)
                softmax_pv(*bufs[cur], 2 * jj, None)
                scores(i, 2 * jj + 2, *bufs[cur], 0)
                softmax_pv(*bufs[1 - cur], 2 * jj + 1, None)
                return carry
            lax.fori_loop(0, i, tile_pair, 0)
        lax.cond(parity == 0, lambda: plain_tiles(0), lambda: plain_tiles(1))
        parity = step(parity, 2 * i, 0, i, 2 * i + 1, tk)
        parity = step(parity, 2 * i + 1, 1, jnp.minimum(i + 1, nq - 1), 0, 0)
        finalize(i)
        return parity

    lax.fori_loop(0, nq, row_body, 0)


def _attention(lam, g2, q, k, v, batch, seq, lambda_init):
    t = q.shape[0]
    tq = ATTN_TILE
    n_pairs = ATTN_WIDTH // LANES
    seq_block = pl.BlockSpec((seq, LANES), lambda b, p: (b, p))
    return pl.pallas_call(
        functools.partial(_attn_body, tq=tq, lambda_init=lambda_init),
        out_shape=jax.ShapeDtypeStruct((t, ATTN_WIDTH), BF16),
        grid=(batch, n_pairs),
        in_specs=[_const_spec(lam.shape), _const_spec((1, LANES)), seq_block, seq_block, seq_block],
        out_specs=seq_block,
        scratch_shapes=[
            pltpu.VMEM((N_MAPS, tq, tq // 2), F32),
            pltpu.VMEM((N_MAPS, tq, tq // 2), F32),
            pltpu.VMEM((N_MAPS, tq, LANES), F32),
            pltpu.VMEM((N_MAPS, tq, LANES), F32),
            pltpu.VMEM((N_MAPS, tq, LANES), F32),
            pltpu.VMEM((N_MAPS, tq, LANES), F32),
        ],
        compiler_params=_params("parallel", "parallel"),
        name="attn",
    )(lam, g2, q, k, v)


def _lane_bcast(per_head, h):
    return jnp.broadcast_to(per_head[:, h:h + 1], (per_head.shape[0], LANES))


def _ssd_body(gz_ref, xa_ref, dt_ref, dtb_ref, alog_ref, dskip_ref, ng_ref, o_ref, st_sc):
    ch = gz_ref.shape[0]
    chunk = pl.program_id(1)

    @pl.when(chunk == 0)
    def _():
        st_sc[...] = jnp.zeros(st_sc.shape, F32)

    xa = xa_ref[...]
    n_tiles = SSM_WIDTH // LANES
    tiles_per_group = n_tiles // SSM_GROUPS
    xs = [xa[:, t * LANES:(t + 1) * LANES] for t in range(n_tiles)]
    bm = [xa[:, SSM_WIDTH + g * SSM_STATE:SSM_WIDTH + (g + 1) * SSM_STATE].astype(BF16) for g in range(SSM_GROUPS)]
    c_off = SSM_WIDTH + SSM_GROUPS * SSM_STATE
    cm = [xa[:, c_off + g * SSM_STATE:c_off + (g + 1) * SSM_STATE].astype(BF16) for g in range(SSM_GROUPS)]

    dt_in = dt_ref[...] + dtb_ref[...]
    dt = jnp.maximum(dt_in, 0.0) + jnp.log1p(jnp.exp(-jnp.abs(dt_in)))
    da = dt * (-jnp.exp(alog_ref[...]))
    row = lax.broadcasted_iota(jnp.int32, (ch, ch), 0)
    colm = lax.broadcasted_iota(jnp.int32, (ch, ch), 1)
    tri = row >= colm
    acs = jnp.dot(tri.astype(F32), da, preferred_element_type=F32, precision=lax.Precision.HIGHEST)
    acs_t = acs.T
    first = lax.broadcasted_iota(jnp.int32, (1, LANES), 1) < SSM_HEAD_DIM

    y_tiles = []
    for g in range(SSM_GROUPS):
        cb = lax.dot_general(cm[g], bm[g], (((1,), (1,)), ((), ())), preferred_element_type=F32)
        st_prev = st_sc[g]
        y_off = jnp.dot(cm[g], st_prev.astype(BF16), preferred_element_type=F32)
        xdec, chunk_decay = [], []
        for tg in range(tiles_per_group):
            t = g * tiles_per_group + tg
            heads = (2 * t, 2 * t + 1)
            acs_b = [_lane_bcast(acs, h) for h in heads]
            acs_pair = jnp.where(first, acs_b[0], acs_b[1])
            dt_pair = jnp.where(first, _lane_bcast(dt, heads[0]), _lane_bcast(dt, heads[1]))
            decay_in = jnp.exp(acs_pair)
            decay_out = jnp.exp(acs_pair[ch - 1:ch, :] - acs_pair)
            xdt = xs[t] * dt_pair
            xdt_b = xdt.astype(BF16)
            halves = []
            for i, h in enumerate(heads):
                seg = acs_b[i] - acs_t[h:h + 1, :]
                gmat = (cb * jnp.exp(jnp.where(tri, seg, -jnp.inf))).astype(BF16)
                halves.append(jnp.dot(gmat, xdt_b, preferred_element_type=F32))
            y_diag = jnp.where(first, halves[0], halves[1])
            y_tiles.append(y_diag + y_off[:, tg * LANES:(tg + 1) * LANES] * decay_in
                           + xs[t] * dskip_ref[:, t * LANES:(t + 1) * LANES])
            xdec.append((xdt * decay_out).astype(BF16))
            chunk_decay.append(decay_in[ch - 1:ch, :])
        xdec_g = jnp.concatenate(xdec, axis=1)
        contrib = lax.dot_general(bm[g], xdec_g, (((0,), (0,)), ((), ())), preferred_element_type=F32)
        st_sc[g] = st_prev * jnp.concatenate(chunk_decay, axis=1) + contrib

    for g in range(SSM_GROUPS):
        yg = jnp.concatenate(y_tiles[g * tiles_per_group:(g + 1) * tiles_per_group], axis=1)
        cols = slice(g * GROUP_WIDTH, (g + 1) * GROUP_WIDTH)
        yg = yg * gz_ref[:, cols]
        o_ref[:, cols] = _rms(yg, ng_ref[:, cols]).astype(o_ref.dtype)


def _ssd(gz, xa, dt, dt_bias, a_log, d_skip, norm_g, batch, seq):
    t = gz.shape[0]
    ch = SSD_CHUNK
    nc = seq // ch
    tok = lambda b, c: (b * nc + c, 0)
    return pl.pallas_call(
        _ssd_body,
        out_shape=jax.ShapeDtypeStruct((t, SSM_WIDTH), BF16),
        grid=(batch, nc),
        in_specs=[
            pl.BlockSpec((ch, SSM_WIDTH), tok),
            pl.BlockSpec((ch, XBC_COLS), tok),
            pl.BlockSpec((ch, LANES), tok),
            _const_spec((1, LANES)),
            _const_spec((1, LANES)),
            _const_spec((1, SSM_WIDTH)),
            _const_spec((1, SSM_WIDTH)),
        ],
        out_specs=pl.BlockSpec((ch, SSM_WIDTH), tok),
        scratch_shapes=[pltpu.VMEM((SSM_GROUPS, SSM_STATE, GROUP_WIDTH), F32)],
        compiler_params=_params("parallel", "arbitrary"),
        name="ssd",
    )(gz, xa, dt, dt_bias, a_log, d_skip, norm_g)


def _outproj_body(x_ref, attn_ref, ssm_ref, wa_ref, ws_ref, g_ref, o_ref):
    mixed = jnp.dot(attn_ref[...], wa_ref[...], preferred_element_type=F32)
    mixed = mixed + jnp.dot(ssm_ref[...], ws_ref[...], preferred_element_type=F32)
    o_ref[...] = x_ref[...] + _rms(mixed, g_ref[...])


def _outproj(x, attn, ssm, wa, ws, g):
    t, d = x.shape
    tm = PROJ_TOKEN_TILE
    return pl.pallas_call(
        _outproj_body,
        out_shape=jax.ShapeDtypeStruct((t, d), F32),
        grid=(t // tm,),
        in_specs=[
            pl.BlockSpec((tm, d), lambda i: (i, 0)),
            pl.BlockSpec((tm, ATTN_WIDTH), lambda i: (i, 0)),
            pl.BlockSpec((tm, SSM_WIDTH), lambda i: (i, 0)),
            _const_spec(wa.shape),
            _const_spec(ws.shape),
            _const_spec((1, d)),
        ],
        out_specs=pl.BlockSpec((tm, d), lambda i: (i, 0)),
        compiler_params=_params("parallel"),
        name="outproj",
    )(x, attn, ssm, wa, ws, g)


def _pad_lanes(row):
    return jnp.pad(row.astype(F32), (0, LANES - row.shape[0]))[None, :]


def _layer(x, p, layer_index, batch, seq):
    lambda_init = 0.8 - 0.6 * math.exp(-0.3 * layer_index)
    row = lambda a: a.astype(F32)[None, :]

    x = _ffn(x, row(p["ffn1_pre_g"]), p["ffn1_w_gate"].astype(BF16), p["ffn1_w_up"].astype(BF16),
             p["ffn1_w_down"].astype(BF16), row(p["ffn1_post_g"]))

    w_in = p["w_in"]
    dt_cols = w_in[:, -SSM_HEADS:]
    w_packed = jnp.concatenate(
        [w_in[:, :-SSM_HEADS], jnp.pad(dt_cols, ((0, 0), (0, LANES - SSM_HEADS)))], axis=1).astype(BF16)
    f32 = lambda a: a.astype(F32)
    q, k, v, gz, xa, dt = _inproj(x, row(p["mix_pre_g"]), w_packed, f32(p["conv_w"]), row(p["conv_b"]), seq)

    lam = jnp.stack([f32(p[n]) for n in ("lambda_q1", "lambda_k1", "lambda_q2", "lambda_k2")])
    g2 = jnp.tile(row(p["attn_subln_g"]), (1, LANES // ATTN_V_DIM))
    attn = _attention(lam, g2, q, k, v, batch, seq, lambda_init)

    d_skip = jnp.repeat(f32(p["d_skip"]), SSM_HEAD_DIM)[None, :]
    ssm = _ssd(gz, xa, dt, _pad_lanes(p["dt_bias"]), _pad_lanes(p["a_log"]), d_skip,
               row(p["ssm_norm_g"]), batch, seq)

    w_out = p["w_out"].astype(BF16)
    x = _outproj(x, attn, ssm, w_out[:ATTN_WIDTH], w_out[ATTN_WIDTH:], row(p["mix_post_g"]))

    return _ffn(x, row(p["ffn2_pre_g"]), p["ffn2_w_gate"].astype(BF16), p["ffn2_w_up"].astype(BF16),
                p["ffn2_w_down"].astype(BF16), row(p["ffn2_post_g"]))


def kernel(x, ffn1_pre_g, ffn1_w_gate, ffn1_w_up, ffn1_w_down, ffn1_post_g, mix_pre_g, w_in, lambda_q1, lambda_k1, lambda_q2, lambda_k2, attn_subln_g, conv_w, conv_b, dt_bias, a_log, d_skip, ssm_norm_g, w_out, mix_post_g, ffn2_pre_g, ffn2_w_gate, ffn2_w_up, ffn2_w_down, ffn2_post_g):
    params = dict(
        ffn1_pre_g=ffn1_pre_g, ffn1_w_gate=ffn1_w_gate, ffn1_w_up=ffn1_w_up, ffn1_w_down=ffn1_w_down,
        ffn1_post_g=ffn1_post_g, mix_pre_g=mix_pre_g, w_in=w_in, lambda_q1=lambda_q1, lambda_k1=lambda_k1,
        lambda_q2=lambda_q2, lambda_k2=lambda_k2, attn_subln_g=attn_subln_g, conv_w=conv_w, conv_b=conv_b,
        dt_bias=dt_bias, a_log=a_log, d_skip=d_skip, ssm_norm_g=ssm_norm_g, w_out=w_out,
        mix_post_g=mix_post_g, ffn2_pre_g=ffn2_pre_g, ffn2_w_gate=ffn2_w_gate, ffn2_w_up=ffn2_w_up,
        ffn2_w_down=ffn2_w_down, ffn2_post_g=ffn2_post_g)
    batch, seq, d = x.shape
    h = x.reshape(batch * seq, d)
    for i in range(ffn1_pre_g.shape[0]):
        h = _layer(h, {name: a[i] for name, a in params.items()}, i, batch, seq)
    return h.reshape(batch, seq, d)
```

```python
import functools
import math

import jax
import jax.numpy as jnp
from jax import lax
from jax.experimental import pallas as pl
from jax.experimental.pallas import tpu as pltpu

F32 = jnp.float32
BF16 = jnp.bfloat16

EPS = 1e-6
LANES = 128
ATTN_HEADS = 8
ATTN_QK_DIM = 32
ATTN_V_DIM = 64
ATTN_WIDTH = ATTN_HEADS * ATTN_V_DIM
SSM_HEADS = 16
SSM_HEAD_DIM = 64
SSM_WIDTH = SSM_HEADS * SSM_HEAD_DIM
SSM_GROUPS = 2
SSM_STATE = 128
CONV_WIDTH = 4
XBC_COLS = SSM_WIDTH + 2 * SSM_GROUPS * SSM_STATE
GROUP_WIDTH = SSM_WIDTH // SSM_GROUPS
NEG_BIG = -1e30

VMEM_LIMIT_BYTES = 56 * 1024 * 1024

FFN_TOKEN_TILE = 512
FFN_FF_CHUNK = 256
PROJ_TOKEN_TILE = 512
ATTN_TILE = 1024
SSD_CHUNK = 128
CONV_HALO = 8


def _rms(xf, g):
    return xf * lax.rsqrt(jnp.mean(xf * xf, axis=-1, keepdims=True) + EPS) * g


def _silu(v):
    return v / (1.0 + jnp.exp(-v))


def _params(*semantics):
    return pltpu.CompilerParams(dimension_semantics=semantics, vmem_limit_bytes=VMEM_LIMIT_BYTES)


def _const_spec(shape):
    return pl.BlockSpec(shape, lambda *_: (0,) * len(shape))


def _ffn_body(x_ref, pre_g_ref, wg_ref, wu_ref, wd_ref, post_g_ref, o_ref):
    x = x_ref[...]
    h = _rms(x, pre_g_ref[...]).astype(BF16)
    d_ff = wg_ref.shape[1]
    y = jnp.zeros(x.shape, F32)
    for c in range(d_ff // FFN_FF_CHUNK):
        cols = slice(c * FFN_FF_CHUNK, (c + 1) * FFN_FF_CHUNK)
        g = jnp.dot(h, wg_ref[:, cols], preferred_element_type=F32)
        u = jnp.dot(h, wu_ref[:, cols], preferred_element_type=F32)
        act = (_silu(g) * u).astype(BF16)
        y = y + jnp.dot(act, wd_ref[cols, :], preferred_element_type=F32)
    o_ref[...] = x + 0.5 * _rms(y, post_g_ref[...])


def _ffn(x, pre_g, wg, wu, wd, post_g):
    t, d = x.shape
    d_ff = wg.shape[1]
    tm = FFN_TOKEN_TILE
    return pl.pallas_call(
        _ffn_body,
        out_shape=jax.ShapeDtypeStruct((t, d), F32),
        grid=(t // tm,),
        in_specs=[
            pl.BlockSpec((tm, d), lambda i: (i, 0)),
            _const_spec((1, d)),
            _const_spec((d, d_ff)),
            _const_spec((d, d_ff)),
            _const_spec((d_ff, d)),
            _const_spec((1, d)),
        ],
        out_specs=pl.BlockSpec((tm, d), lambda i: (i, 0)),
        compiler_params=_params("parallel"),
        name="ffn",
    )(x, pre_g, wg, wu, wd, post_g)


def _inproj_body(x_ref, g_ref, w_ref, convw_ref, convb_ref, q_ref, k_ref, v_ref, gz_ref, xa_ref, dt_ref,
                 ubuf, *, tiles_per_seq):
    tm = x_ref.shape[0]

    @pl.when(pl.program_id(0) % tiles_per_seq == 0)
    def _():
        ubuf[0:CONV_HALO, :] = jnp.zeros((CONV_HALO, XBC_COLS), F32)

    h = _rms(x_ref[...], g_ref[...]).astype(BF16)
    offsets = {}
    col = 0
    for name, width in (("q", ATTN_WIDTH), ("k", ATTN_WIDTH), ("v", ATTN_WIDTH), ("z", SSM_WIDTH),
                        ("xbc", XBC_COLS), ("dt", LANES)):
        offsets[name] = (col, col + width)
        col += width

    def proj(name):
        lo, hi = offsets[name]
        return jnp.dot(h, w_ref[:, lo:hi], preferred_element_type=F32)

    def conv_chunk(c):
        width = XBC_COLS // 3
        cols = slice(c * width, (c + 1) * width)
        lo = offsets["xbc"][0] + c * width
        ubuf[CONV_HALO:CONV_HALO + tm, cols] = jnp.dot(h, w_ref[:, lo:lo + width], preferred_element_type=F32)
        conv = convb_ref[:, cols]
        for tap in range(CONV_WIDTH):
            shift = CONV_WIDTH - 1 - tap
            conv = conv + convw_ref[tap:tap + 1, cols] * ubuf[CONV_HALO - shift:CONV_HALO - shift + tm, cols]
        ubuf[0:CONV_HALO, cols] = ubuf[tm:tm + CONV_HALO, cols]
        xa_ref[:, cols] = _silu(conv)

    conv_chunk(0)
    q_ref[...] = (proj("q") * (ATTN_QK_DIM ** -0.5 * math.log2(math.e))).astype(BF16)
    k_ref[...] = proj("k").astype(BF16)
    conv_chunk(1)
    v_ref[...] = proj("v").astype(BF16)

    def gate_half(c):
        half = SSM_WIDTH // 2
        lo = offsets["z"][0] + c * half
        z = jnp.dot(h, w_ref[:, lo:lo + half], preferred_element_type=F32)
        gz_ref[:, c * half:(c + 1) * half] = _silu(z)

    gate_half(0)
    conv_chunk(2)
    gate_half(1)
    dt_ref[...] = proj("dt")


def _inproj(x, g, w, conv_w, conv_b, seq):
    t, d = x.shape
    tm = PROJ_TOKEN_TILE
    widths = (ATTN_WIDTH, ATTN_WIDTH, ATTN_WIDTH, SSM_WIDTH, XBC_COLS, LANES)
    dtypes = (BF16, BF16, BF16, F32, F32, F32)
    return pl.pallas_call(
        functools.partial(_inproj_body, tiles_per_seq=seq // tm),
        out_shape=[jax.ShapeDtypeStruct((t, n), dt) for n, dt in zip(widths, dtypes)],
        grid=(t // tm,),
        in_specs=[
            pl.BlockSpec((tm, d), lambda i: (i, 0)),
            _const_spec((1, d)),
            _const_spec(w.shape),
            _const_spec((CONV_WIDTH, XBC_COLS)),
            _const_spec((1, XBC_COLS)),
        ],
        out_specs=[pl.BlockSpec((tm, n), lambda i: (i, 0)) for n in widths],
        scratch_shapes=[pltpu.VMEM((CONV_HALO + tm, XBC_COLS), F32)],
        compiler_params=_params("arbitrary"),
        name="inproj",
    )(x, g, w, conv_w, conv_b)


N_MAPS = 4


def _attn_body(lam_ref, g_ref, q_ref, k_ref, v_ref, o_ref, s0_sc, s1_sc, mx0_sc, mx1_sc, m_sc, acc_sc, *,
               tq, lambda_init):
    seq = q_ref.shape[0]
    tk = tq // 2
    nq = seq // tq
    n_lane_tiles = tk // LANES
    pair = pl.program_id(1)
    log2e = math.log2(math.e)

    lane = lax.broadcasted_iota(jnp.int32, (1, LANES), 1)
    first = lane < ATTN_V_DIM
    head_a = (2 * pair + 1).astype(F32)
    slope = [jnp.exp2(-(head_a + (m // 2))) * log2e for m in range(N_MAPS)]
    col_pos = lax.broadcasted_iota(jnp.int32, (1, LANES), 1).astype(F32)

    def scores(row, tile, s_ref, mx_ref, row_lo):
        q = q_ref[pl.ds(pl.multiple_of(row * tq + row_lo, tk), tq - row_lo), :]
        kt = k_ref[pl.ds(pl.multiple_of(tile * tk, tk), tk), :]
        rel = lax.convert_element_type(tile * tk - row * tq, F32)
        for m in range(N_MAPS):
            qm = jnp.where((lane >= 32 * m) & (lane < 32 * (m + 1)), q, jnp.zeros_like(q))
            s = lax.dot_general(qm, kt, (((1,), (1,)), ((), ())), preferred_element_type=F32)
            tile_max = None
            for t in range(n_lane_tiles):
                st = s[:, t * LANES:(t + 1) * LANES] + slope[m] * (col_pos + (rel + t * LANES))
                s_ref[m, row_lo:tq, t * LANES:(t + 1) * LANES] = st
                tile_max = st if tile_max is None else jnp.maximum(tile_max, st)
            mx_ref[m, row_lo:tq, :] = tile_max

    def softmax_pv(s_ref, mx_ref, tile, diag):
        row_lo = tk if diag == 1 else 0
        rs = slice(row_lo, tq)
        vt = v_ref[pl.ds(pl.multiple_of(tile * tk, tk), tk), :]
        ones = jnp.ones_like(vt)
        v_heads = (jnp.where(first, vt, ones), jnp.where(first, ones, vt))

        def load(m, t):
            st = s_ref[m, rs, t * LANES:(t + 1) * LANES]
            if diag is None:
                return st
            shape = (tq - row_lo, LANES)
            visible = (lax.broadcasted_iota(jnp.int32, shape, 0) + row_lo
                       >= lax.broadcasted_iota(jnp.int32, shape, 1) + (diag * tk + t * LANES))
            return jnp.where(visible, st, -jnp.inf)

        for m in range(N_MAPS):
            if diag is None:
                tile_max = mx_ref[m, rs, :]
            else:
                tile_max = load(m, 0)
                for t in range(1, n_lane_tiles):
                    tile_max = jnp.maximum(tile_max, load(m, t))
            m_old = m_sc[m, rs, :]
            m_new = jnp.maximum(m_old, jnp.max(tile_max, axis=1, keepdims=True))
            m_sc[m, rs, :] = m_new
            acc_sc[m, rs, :] = acc_sc[m, rs, :] * jnp.exp2(m_old - m_new)
        for m in range(N_MAPS):
            p = jnp.concatenate(
                [jnp.exp2(load(m, t) - m_sc[m, rs, :]).astype(BF16) for t in range(n_lane_tiles)], axis=1)
            acc_sc[m, rs, :] += jnp.dot(p, v_heads[m // 2], preferred_element_type=F32)

    def finalize(row):
        lp = lam_ref[...]
        lam = (jnp.exp(jnp.sum(lp[0:1] * lp[1:2], axis=1, keepdims=True))
               - jnp.exp(jnp.sum(lp[2:3] * lp[3:4], axis=1, keepdims=True)) + lambda_init)
        ratio = [acc_sc[m] / pltpu.roll(acc_sc[m], ATTN_V_DIM, 1) for m in range(N_MAPS)]
        o = jnp.where(first, ratio[0] - lam * ratio[1], ratio[2] - lam * ratio[3])
        sq = o * o
        ss_a = jnp.sum(jnp.where(first, sq, 0.0), axis=1, keepdims=True)
        ss_b = jnp.sum(jnp.where(first, 0.0, sq), axis=1, keepdims=True)
        ms = jnp.where(first, ss_a, ss_b) * (1.0 / ATTN_V_DIM)
        o = o * lax.rsqrt(ms + EPS) * g_ref[...] * (1.0 - lambda_init)
        o_ref[pl.ds(pl.multiple_of(row * tq, tq), tq), :] = o.astype(o_ref.dtype)

    buf_a, buf_b = (s0_sc, mx0_sc), (s1_sc, mx1_sc)

    def reset_stats():
        m_sc[...] = jnp.full(m_sc.shape, NEG_BIG, F32)
        acc_sc[...] = jnp.zeros(acc_sc.shape, F32)

    scores(0, 0, *buf_a, 0)
    reset_stats()

    def row_body(i, carry):
        def tile_pair(jj, inner):
            scores(i, 2 * jj + 1, *buf_b, 0)
            softmax_pv(*buf_a, 2 * jj, None)
            scores(i, 2 * jj + 2, *buf_a, 0)
            softmax_pv(*buf_b, 2 * jj + 1, None)
            return inner

        lax.fori_loop(0, i, tile_pair, 0)
        scores(i, 2 * i + 1, *buf_b, tk)
        softmax_pv(*buf_a, 2 * i, 0)
        scores(jnp.minimum(i + 1, nq - 1), 0, *buf_a, 0)
        softmax_pv(*buf_b, 2 * i + 1, 1)
        finalize(i)
        reset_stats()
        return carry

    lax.fori_loop(0, nq, row_body, 0)


def _attention(lam, g2, q, k, v, batch, seq, lambda_init):
    t = q.shape[0]
    tq = ATTN_TILE
    n_pairs = ATTN_WIDTH // LANES
    seq_block = pl.BlockSpec((seq, LANES), lambda b, p: (b, p))
    return pl.pallas_call(
        functools.partial(_attn_body, tq=tq, lambda_init=lambda_init),
        out_shape=jax.ShapeDtypeStruct((t, ATTN_WIDTH), BF16),
        grid=(batch, n_pairs),
        in_specs=[_const_spec(lam.shape), _const_spec((1, LANES)), seq_block, seq_block, seq_block],
        out_specs=seq_block,
        scratch_shapes=[
            pltpu.VMEM((N_MAPS, tq, tq // 2), F32),
            pltpu.VMEM((N_MAPS, tq, tq // 2), F32),
            pltpu.VMEM((N_MAPS, tq, LANES), F32),
            pltpu.VMEM((N_MAPS, tq, LANES), F32),
            pltpu.VMEM((N_MAPS, tq, LANES), F32),
            pltpu.VMEM((N_MAPS, tq, LANES), F32),
        ],
        compiler_params=_params("parallel", "parallel"),
        name="attn",
    )(lam, g2, q, k, v)


def _lane_bcast(per_head, h):
    return jnp.broadcast_to(per_head[:, h:h + 1], (per_head.shape[0], LANES))


def _ssd_body(gz_ref, xa_ref, dt_ref, dtb_ref, alog_ref, dskip_ref, ng_ref, o_ref, st_sc):
    ch = gz_ref.shape[0]
    chunk = pl.program_id(1)

    @pl.when(chunk == 0)
    def _():
        st_sc[...] = jnp.zeros(st_sc.shape, F32)

    xa = xa_ref[...]
    n_tiles = SSM_WIDTH // LANES
    tiles_per_group = n_tiles // SSM_GROUPS
    xs = [xa[:, t * LANES:(t + 1) * LANES] for t in range(n_tiles)]
    bm = [xa[:, SSM_WIDTH + g * SSM_STATE:SSM_WIDTH + (g + 1) * SSM_STATE].astype(BF16) for g in range(SSM_GROUPS)]
    c_off = SSM_WIDTH + SSM_GROUPS * SSM_STATE
    cm = [xa[:, c_off + g * SSM_STATE:c_off + (g + 1) * SSM_STATE].astype(BF16) for g in range(SSM_GROUPS)]

    dt_in = dt_ref[...] + dtb_ref[...]
    dt = jnp.maximum(dt_in, 0.0) + jnp.log1p(jnp.exp(-jnp.abs(dt_in)))
    da = dt * (-jnp.exp(alog_ref[...]))
    row = lax.broadcasted_iota(jnp.int32, (ch, ch), 0)
    colm = lax.broadcasted_iota(jnp.int32, (ch, ch), 1)
    tri = row >= colm
    acs = jnp.dot(tri.astype(F32), da, preferred_element_type=F32, precision=lax.Precision.HIGHEST)
    acs_t = acs.T
    first = lax.broadcasted_iota(jnp.int32, (1, LANES), 1) < SSM_HEAD_DIM

    y_tiles = []
    for g in range(SSM_GROUPS):
        cb = lax.dot_general(cm[g], bm[g], (((1,), (1,)), ((), ())), preferred_element_type=F32)
        st_prev = st_sc[g]
        y_off = jnp.dot(cm[g], st_prev.astype(BF16), preferred_element_type=F32)
        xdec, chunk_decay = [], []
        for tg in range(tiles_per_group):
            t = g * tiles_per_group + tg
            heads = (2 * t, 2 * t + 1)
            acs_b = [_lane_bcast(acs, h) for h in heads]
            acs_pair = jnp.where(first, acs_b[0], acs_b[1])
            dt_pair = jnp.where(first, _lane_bcast(dt, heads[0]), _lane_bcast(dt, heads[1]))
            decay_in = jnp.exp(acs_pair)
            decay_out = jnp.exp(acs_pair[ch - 1:ch, :] - acs_pair)
            xdt = xs[t] * dt_pair
            xdt_b = xdt.astype(BF16)
            halves = []
            for i, h in enumerate(heads):
                seg = acs_b[i] - acs_t[h:h + 1, :]
                gmat = (cb * jnp.exp(jnp.where(tri, seg, -jnp.inf))).astype(BF16)
                halves.append(jnp.dot(gmat, xdt_b, preferred_element_type=F32))
            y_diag = jnp.where(first, halves[0], halves[1])
            y_tiles.append(y_diag + y_off[:, tg * LANES:(tg + 1) * LANES] * decay_in
                           + xs[t] * dskip_ref[:, t * LANES:(t + 1) * LANES])
            xdec.append((xdt * decay_out).astype(BF16))
            chunk_decay.append(decay_in[ch - 1:ch, :])
        xdec_g = jnp.concatenate(xdec, axis=1)
        contrib = lax.dot_general(bm[g], xdec_g, (((0,), (0,)), ((), ())), preferred_element_type=F32)
        st_sc[g] = st_prev * jnp.concatenate(chunk_decay, axis=1) + contrib

    for g in range(SSM_GROUPS):
        yg = jnp.concatenate(y_tiles[g * tiles_per_group:(g + 1) * tiles_per_group], axis=1)
        cols = slice(g * GROUP_WIDTH, (g + 1) * GROUP_WIDTH)
        yg = yg * gz_ref[:, cols]
        o_ref[:, cols] = _rms(yg, ng_ref[:, cols]).astype(o_ref.dtype)


def _ssd(gz, xa, dt, dt_bias, a_log, d_skip, norm_g, batch, seq):
    t = gz.shape[0]
    ch = SSD_CHUNK
    nc = seq // ch
    tok = lambda b, c: (b * nc + c, 0)
    return pl.pallas_call(
        _ssd_body,
        out_shape=jax.ShapeDtypeStruct((t, SSM_WIDTH), BF16),
        grid=(batch, nc),
        in_specs=[
            pl.BlockSpec((ch, SSM_WIDTH), tok),
            pl.BlockSpec((ch, XBC_COLS), tok),
            pl.BlockSpec((ch, LANES), tok),
            _const_spec((1, LANES)),
            _const_spec((1, LANES)),
            _const_spec((1, SSM_WIDTH)),
            _const_spec((1, SSM_WIDTH)),
        ],
        out_specs=pl.BlockSpec((ch, SSM_WIDTH), tok),
        scratch_shapes=[pltpu.VMEM((SSM_GROUPS, SSM_STATE, GROUP_WIDTH), F32)],
        compiler_params=_params("parallel", "arbitrary"),
        name="ssd",
    )(gz, xa, dt, dt_bias, a_log, d_skip, norm_g)


def _outproj_body(x_ref, attn_ref, ssm_ref, wa_ref, ws_ref, g_ref, o_ref):
    mixed = jnp.dot(attn_ref[...], wa_ref[...], preferred_element_type=F32)
    mixed = mixed + jnp.dot(ssm_ref[...], ws_ref[...], preferred_element_type=F32)
    o_ref[...] = x_ref[...] + _rms(mixed, g_ref[...])


def _outproj(x, attn, ssm, wa, ws, g):
    t, d = x.shape
    tm = PROJ_TOKEN_TILE
    return pl.pallas_call(
        _outproj_body,
        out_shape=jax.ShapeDtypeStruct((t, d), F32),
        grid=(t // tm,),
        in_specs=[
            pl.BlockSpec((tm, d), lambda i: (i, 0)),
            pl.BlockSpec((tm, ATTN_WIDTH), lambda i: (i, 0)),
            pl.BlockSpec((tm, SSM_WIDTH), lambda i: (i, 0)),
            _const_spec(wa.shape),
            _const_spec(ws.shape),
            _const_spec((1, d)),
        ],
        out_specs=pl.BlockSpec((tm, d), lambda i: (i, 0)),
        compiler_params=_params("parallel"),
        name="outproj",
    )(x, attn, ssm, wa, ws, g)


def _pad_lanes(row):
    return jnp.pad(row.astype(F32), (0, LANES - row.shape[0]))[None, :]


def _layer(x, p, layer_index, batch, seq):
    lambda_init = 0.8 - 0.6 * math.exp(-0.3 * layer_index)
    row = lambda a: a.astype(F32)[None, :]

    x = _ffn(x, row(p["ffn1_pre_g"]), p["ffn1_w_gate"].astype(BF16), p["ffn1_w_up"].astype(BF16),
             p["ffn1_w_down"].astype(BF16), row(p["ffn1_post_g"]))

    w_in = p["w_in"]
    dt_cols = w_in[:, -SSM_HEADS:]
    w_packed = jnp.concatenate(
        [w_in[:, :-SSM_HEADS], jnp.pad(dt_cols, ((0, 0), (0, LANES - SSM_HEADS)))], axis=1).astype(BF16)
    f32 = lambda a: a.astype(F32)
    q, k, v, gz, xa, dt = _inproj(x, row(p["mix_pre_g"]), w_packed, f32(p["conv_w"]), row(p["conv_b"]), seq)

    lam = jnp.stack([f32(p[n]) for n in ("lambda_q1", "lambda_k1", "lambda_q2", "lambda_k2")])
    g2 = jnp.tile(row(p["attn_subln_g"]), (1, LANES // ATTN_V_DIM))
    attn = _attention(lam, g2, q, k, v, batch, seq, lambda_init)

    d_skip = jnp.repeat(f32(p["d_skip"]), SSM_HEAD_DIM)[None, :]
    ssm = _ssd(gz, xa, dt, _pad_lanes(p["dt_bias"]), _pad_lanes(p["a_log"]), d_skip,
               row(p["ssm_norm_g"]), batch, seq)

    w_out = p["w_out"].astype(BF16)
    x = _outproj(x, attn, ssm, w_out[:ATTN_WIDTH], w_out[ATTN_WIDTH:], row(p["mix_post_g"]))

    return _ffn(x, row(p["ffn2_pre_g"]), p["ffn2_w_gate"].astype(BF16), p["ffn2_w_up"].astype(BF16),
                p["ffn2_w_down"].astype(BF16), row(p["ffn2_post_g"]))


def kernel(x, ffn1_pre_g, ffn1_w_gate, ffn1_w_up, ffn1_w_down, ffn1_post_g, mix_pre_g, w_in, lambda_q1, lambda_k1, lambda_q2, lambda_k2, attn_subln_g, conv_w, conv_b, dt_bias, a_log, d_skip, ssm_norm_g, w_out, mix_post_g, ffn2_pre_g, ffn2_w_gate, ffn2_w_up, ffn2_w_down, ffn2_post_g):
    params = dict(
        ffn1_pre_g=ffn1_pre_g, ffn1_w_gate=ffn1_w_gate, ffn1_w_up=ffn1_w_up, ffn1_w_down=ffn1_w_down,
        ffn1_post_g=ffn1_post_g, mix_pre_g=mix_pre_g, w_in=w_in, lambda_q1=lambda_q1, lambda_k1=lambda_k1,
        lambda_q2=lambda_q2, lambda_k2=lambda_k2, attn_subln_g=attn_subln_g, conv_w=conv_w, conv_b=conv_b,
        dt_bias=dt_bias, a_log=a_log, d_skip=d_skip, ssm_norm_g=ssm_norm_g, w_out=w_out,
        mix_post_g=mix_post_g, ffn2_pre_g=ffn2_pre_g, ffn2_w_gate=ffn2_w_gate, ffn2_w_up=ffn2_w_up,
        ffn2_w_down=ffn2_w_down, ffn2_post_g=ffn2_post_g)
    batch, seq, d = x.shape
    h = x.reshape(batch * seq, d)
    for i in range(ffn1_pre_g.shape[0]):
        h = _layer(h, {name: a[i] for name, a in params.items()}, i, batch, seq)
    return h.reshape(batch, seq, d)
```

```python
import functools
import math

import jax
import jax.numpy as jnp
from jax import lax
from jax.experimental import pallas as pl
from jax.experimental.pallas import tpu as pltpu

F32 = jnp.float32
BF16 = jnp.bfloat16

EPS = 1e-6
LANES = 128
ATTN_HEADS = 8
ATTN_QK_DIM = 32
ATTN_V_DIM = 64
ATTN_WIDTH = ATTN_HEADS * ATTN_V_DIM
SSM_HEADS = 16
SSM_HEAD_DIM = 64
SSM_WIDTH = SSM_HEADS * SSM_HEAD_DIM
SSM_GROUPS = 2
SSM_STATE = 128
CONV_WIDTH = 4
XBC_COLS = SSM_WIDTH + 2 * SSM_GROUPS * SSM_STATE
GROUP_WIDTH = SSM_WIDTH // SSM_GROUPS
NEG_BIG = -1e30

VMEM_LIMIT_BYTES = 56 * 1024 * 1024

FFN_TOKEN_TILE = 512
FFN_FF_CHUNK = 256
PROJ_TOKEN_TILE = 512
ATTN_TILE = 1024
SSD_CHUNK = 128
CONV_HALO = 8


def _rms(xf, g):
    return xf * lax.rsqrt(jnp.mean(xf * xf, axis=-1, keepdims=True) + EPS) * g


def _silu(v):
    return v / (1.0 + jnp.exp(-v))


def _params(*semantics):
    return pltpu.CompilerParams(dimension_semantics=semantics, vmem_limit_bytes=VMEM_LIMIT_BYTES)


def _const_spec(shape):
    return pl.BlockSpec(shape, lambda *_: (0,) * len(shape))


def _ffn_body(x_ref, pre_g_ref, wg_ref, wu_ref, wd_ref, post_g_ref, o_ref):
    x = x_ref[...]
    h = _rms(x, pre_g_ref[...]).astype(BF16)
    d_ff = wg_ref.shape[1]
    y = jnp.zeros(x.shape, F32)
    for c in range(d_ff // FFN_FF_CHUNK):
        cols = slice(c * FFN_FF_CHUNK, (c + 1) * FFN_FF_CHUNK)
        g = jnp.dot(h, wg_ref[:, cols], preferred_element_type=F32)
        u = jnp.dot(h, wu_ref[:, cols], preferred_element_type=F32)
        act = (_silu(g) * u).astype(BF16)
        y = y + jnp.dot(act, wd_ref[cols, :], preferred_element_type=F32)
    o_ref[...] = x + 0.5 * _rms(y, post_g_ref[...])


def _ffn(x, pre_g, wg, wu, wd, post_g):
    t, d = x.shape
    d_ff = wg.shape[1]
    tm = FFN_TOKEN_TILE
    return pl.pallas_call(
        _ffn_body,
        out_shape=jax.ShapeDtypeStruct((t, d), F32),
        grid=(t // tm,),
        in_specs=[
            pl.BlockSpec((tm, d), lambda i: (i, 0)),
            _const_spec((1, d)),
            _const_spec((d, d_ff)),
            _const_spec((d, d_ff)),
            _const_spec((d_ff, d)),
            _const_spec((1, d)),
        ],
        out_specs=pl.BlockSpec((tm, d), lambda i: (i, 0)),
        compiler_params=_params("parallel"),
        name="ffn",
    )(x, pre_g, wg, wu, wd, post_g)


def _inproj_body(x_ref, g_ref, w_ref, convw_ref, convb_ref, q_ref, k_ref, v_ref, gz_ref, xa_ref, dt_ref,
                 ubuf, *, tiles_per_seq):
    tm = x_ref.shape[0]

    @pl.when(pl.program_id(0) % tiles_per_seq == 0)
    def _():
        ubuf[0:CONV_HALO, :] = jnp.zeros((CONV_HALO, XBC_COLS), F32)

    h = _rms(x_ref[...], g_ref[...]).astype(BF16)
    offsets = {}
    col = 0
    for name, width in (("q", ATTN_WIDTH), ("k", ATTN_WIDTH), ("v", ATTN_WIDTH), ("z", SSM_WIDTH),
                        ("xbc", XBC_COLS), ("dt", LANES)):
        offsets[name] = (col, col + width)
        col += width

    def proj(name):
        lo, hi = offsets[name]
        return jnp.dot(h, w_ref[:, lo:hi], preferred_element_type=F32)

    def conv_chunk(c):
        width = XBC_COLS // 3
        cols = slice(c * width, (c + 1) * width)
        lo = offsets["xbc"][0] + c * width
        ubuf[CONV_HALO:CONV_HALO + tm, cols] = jnp.dot(h, w_ref[:, lo:lo + width], preferred_element_type=F32)
        u = ubuf[:, cols]
        acc = convw_ref[0:1, cols] * u
        for tap in range(1, CONV_WIDTH):
            acc = pltpu.roll(acc, 1, 0) + convw_ref[tap:tap + 1, cols] * u
        conv = acc[CONV_HALO:CONV_HALO + tm, :] + convb_ref[:, cols]
        ubuf[0:CONV_HALO, cols] = ubuf[tm:tm + CONV_HALO, cols]
        xa_ref[:, cols] = _silu(conv)

    conv_chunk(0)
    q_ref[...] = (proj("q") * (ATTN_QK_DIM ** -0.5 * math.log2(math.e))).astype(BF16)
    k_ref[...] = proj("k").astype(BF16)
    conv_chunk(1)
    v_ref[...] = proj("v").astype(BF16)

    def gate_half(c):
        half = SSM_WIDTH // 2
        lo = offsets["z"][0] + c * half
        z = jnp.dot(h, w_ref[:, lo:lo + half], preferred_element_type=F32)
        gz_ref[:, c * half:(c + 1) * half] = _silu(z)

    gate_half(0)
    conv_chunk(2)
    gate_half(1)
    dt_ref[...] = proj("dt")


def _inproj(x, g, w, conv_w, conv_b, seq):
    t, d = x.shape
    tm = PROJ_TOKEN_TILE
    widths = (ATTN_WIDTH, ATTN_WIDTH, ATTN_WIDTH, SSM_WIDTH, XBC_COLS, LANES)
    dtypes = (BF16, BF16, BF16, F32, F32, F32)
    return pl.pallas_call(
        functools.partial(_inproj_body, tiles_per_seq=seq // tm),
        out_shape=[jax.ShapeDtypeStruct((t, n), dt) for n, dt in zip(widths, dtypes)],
        grid=(t // tm,),
        in_specs=[
            pl.BlockSpec((tm, d), lambda i: (i, 0)),
            _const_spec((1, d)),
            _const_spec(w.shape),
            _const_spec((CONV_WIDTH, XBC_COLS)),
            _const_spec((1, XBC_COLS)),
        ],
        out_specs=[pl.BlockSpec((tm, n), lambda i: (i, 0)) for n in widths],
        scratch_shapes=[pltpu.VMEM((CONV_HALO + tm, XBC_COLS), F32)],
        compiler_params=_params("arbitrary"),
        name="inproj",
    )(x, g, w, conv_w, conv_b)


N_MAPS = 4


def _attn_body(lam_ref, g_ref, q_ref, k_ref, v_ref, o_ref, s0_sc, s1_sc, mx0_sc, mx1_sc, m_sc, acc_sc, *,
               tq, lambda_init):
    seq = q_ref.shape[0]
    tk = tq // 2
    nq = seq // tq
    n_lane_tiles = tk // LANES
    pair = pl.program_id(1)
    log2e = math.log2(math.e)

    lane = lax.broadcasted_iota(jnp.int32, (1, LANES), 1)
    first = lane < ATTN_V_DIM
    head_a = (2 * pair + 1).astype(F32)
    slope = [jnp.exp2(-(head_a + (m // 2))) * log2e for m in range(N_MAPS)]
    col_pos = lax.broadcasted_iota(jnp.int32, (1, LANES), 1).astype(F32)

    def scores(row, tile, s_ref, mx_ref, row_lo):
        q = q_ref[pl.ds(pl.multiple_of(row * tq + row_lo, tk), tq - row_lo), :]
        kt = k_ref[pl.ds(pl.multiple_of(tile * tk, tk), tk), :]
        rel = lax.convert_element_type(tile * tk - row * tq, F32)
        for m in range(N_MAPS):
            qm = jnp.where((lane >= 32 * m) & (lane < 32 * (m + 1)), q, jnp.zeros_like(q))
            s = lax.dot_general(qm, kt, (((1,), (1,)), ((), ())), preferred_element_type=F32)
            tile_max = None
            for t in range(n_lane_tiles):
                st = s[:, t * LANES:(t + 1) * LANES] + slope[m] * (col_pos + (rel + t * LANES))
                s_ref[m, row_lo:tq, t * LANES:(t + 1) * LANES] = st
                tile_max = st if tile_max is None else jnp.maximum(tile_max, st)
            mx_ref[m, row_lo:tq, :] = tile_max

    def softmax_pv(s_ref, mx_ref, tile, diag):
        row_lo = tk if diag == 1 else 0
        rs = slice(row_lo, tq)
        vt = v_ref[pl.ds(pl.multiple_of(tile * tk, tk), tk), :]
        ones = jnp.ones_like(vt)
        v_heads = (jnp.where(first, vt, ones), jnp.where(first, ones, vt))

        def load(m, t):
            st = s_ref[m, rs, t * LANES:(t + 1) * LANES]
            if diag is None:
                return st
            shape = (tq - row_lo, LANES)
            visible = (lax.broadcasted_iota(jnp.int32, shape, 0) + row_lo
                       >= lax.broadcasted_iota(jnp.int32, shape, 1) + (diag * tk + t * LANES))
            return jnp.where(visible, st, -jnp.inf)

        for m in range(N_MAPS):
            if diag is None:
                tile_max = mx_ref[m, rs, :]
            else:
                tile_max = load(m, 0)
                for t in range(1, n_lane_tiles):
                    tile_max = jnp.maximum(tile_max, load(m, t))
            m_old = m_sc[m, rs, :]
            m_new = jnp.maximum(m_old, jnp.max(tile_max, axis=1, keepdims=True))
            m_sc[m, rs, :] = m_new
            acc_sc[m, rs, :] = acc_sc[m, rs, :] * jnp.exp2(m_old - m_new)
        for m in range(N_MAPS):
            p = jnp.concatenate(
                [jnp.exp2(load(m, t) - m_sc[m, rs, :]).astype(BF16) for t in range(n_lane_tiles)], axis=1)
            acc_sc[m, rs, :] += jnp.dot(p, v_heads[m // 2], preferred_element_type=F32)

    def finalize(row):
        lp = lam_ref[...]
        lam = (jnp.exp(jnp.sum(lp[0:1] * lp[1:2], axis=1, keepdims=True))
               - jnp.exp(jnp.sum(lp[2:3] * lp[3:4], axis=1, keepdims=True)) + lambda_init)
        ratio = [acc_sc[m] / pltpu.roll(acc_sc[m], ATTN_V_DIM, 1) for m in range(N_MAPS)]
        o = jnp.where(first, ratio[0] - lam * ratio[1], ratio[2] - lam * ratio[3])
        sq = o * o
        ss_a = jnp.sum(jnp.where(first, sq, 0.0), axis=1, keepdims=True)
        ss_b = jnp.sum(jnp.where(first, 0.0, sq), axis=1, keepdims=True)
        ms = jnp.where(first, ss_a, ss_b) * (1.0 / ATTN_V_DIM)
        o = o * lax.rsqrt(ms + EPS) * g_ref[...] * (1.0 - lambda_init)
        o_ref[pl.ds(pl.multiple_of(row * tq, tq), tq), :] = o.astype(o_ref.dtype)

    buf_a, buf_b = (s0_sc, mx0_sc), (s1_sc, mx1_sc)

    def reset_stats():
        m_sc[...] = jnp.full(m_sc.shape, NEG_BIG, F32)
        acc_sc[...] = jnp.zeros(acc_sc.shape, F32)

    scores(0, 0, *buf_a, 0)
    reset_stats()

    def row_body(i, carry):
        def tile_pair(jj, inner):
            scores(i, 2 * jj + 1, *buf_b, 0)
            softmax_pv(*buf_a, 2 * jj, None)
            scores(i, 2 * jj + 2, *buf_a, 0)
            softmax_pv(*buf_b, 2 * jj + 1, None)
            return inner

        lax.fori_loop(0, i, tile_pair, 0)
        scores(i, 2 * i + 1, *buf_b, tk)
        softmax_pv(*buf_a, 2 * i, 0)
        scores(jnp.minimum(i + 1, nq - 1), 0, *buf_a, 0)
        softmax_pv(*buf_b, 2 * i + 1, 1)
        finalize(i)
        reset_stats()
        return carry

    lax.fori_loop(0, nq, row_body, 0)


def _attention(lam, g2, q, k, v, batch, seq, lambda_init):
    t = q.shape[0]
    tq = ATTN_TILE
    n_pairs = ATTN_WIDTH // LANES
    seq_block = pl.BlockSpec((seq, LANES), lambda b, p: (b, p))
    return pl.pallas_call(
        functools.partial(_attn_body, tq=tq, lambda_init=lambda_init),
        out_shape=jax.ShapeDtypeStruct((t, ATTN_WIDTH), BF16),
        grid=(batch, n_pairs),
        in_specs=[_const_spec(lam.shape), _const_spec((1, LANES)), seq_block, seq_block, seq_block],
        out_specs=seq_block,
        scratch_shapes=[
            pltpu.VMEM((N_MAPS, tq, tq // 2), F32),
            pltpu.VMEM((N_MAPS, tq, tq // 2), F32),
            pltpu.VMEM((N_MAPS, tq, LANES), F32),
            pltpu.VMEM((N_MAPS, tq, LANES), F32),
            pltpu.VMEM((N_MAPS, tq, LANES), F32),
            pltpu.VMEM((N_MAPS, tq, LANES), F32),
        ],
        compiler_params=_params("parallel", "parallel"),
        name="attn",
    )(lam, g2, q, k, v)


def _lane_bcast(per_head, h):
    return jnp.broadcast_to(per_head[:, h:h + 1], (per_head.shape[0], LANES))


def _ssd_body(gz_ref, xa_ref, dt_ref, dtb_ref, alog_ref, dskip_ref, ng_ref, o_ref, st_sc):
    ch = gz_ref.shape[0]
    chunk = pl.program_id(1)

    @pl.when(chunk == 0)
    def _():
        st_sc[...] = jnp.zeros(st_sc.shape, F32)

    xa = xa_ref[...]
    n_tiles = SSM_WIDTH // LANES
    tiles_per_group = n_tiles // SSM_GROUPS
    xs = [xa[:, t * LANES:(t + 1) * LANES] for t in range(n_tiles)]
    bm = [xa[:, SSM_WIDTH + g * SSM_STATE:SSM_WIDTH + (g + 1) * SSM_STATE].astype(BF16) for g in range(SSM_GROUPS)]
    c_off = SSM_WIDTH + SSM_GROUPS * SSM_STATE
    cm = [xa[:, c_off + g * SSM_STATE:c_off + (g + 1) * SSM_STATE].astype(BF16) for g in range(SSM_GROUPS)]

    dt_in = dt_ref[...] + dtb_ref[...]
    dt = jnp.maximum(dt_in, 0.0) + jnp.log1p(jnp.exp(-jnp.abs(dt_in)))
    da = dt * (-jnp.exp(alog_ref[...]))
    row = lax.broadcasted_iota(jnp.int32, (ch, ch), 0)
    colm = lax.broadcasted_iota(jnp.int32, (ch, ch), 1)
    tri = row >= colm
    acs = jnp.dot(tri.astype(F32), da, preferred_element_type=F32, precision=lax.Precision.HIGHEST)
    acs_t = acs.T
    first = lax.broadcasted_iota(jnp.int32, (1, LANES), 1) < SSM_HEAD_DIM

    y_tiles = []
    for g in range(SSM_GROUPS):
        cb = lax.dot_general(cm[g], bm[g], (((1,), (1,)), ((), ())), preferred_element_type=F32)
        st_prev = st_sc[g]
        y_off = jnp.dot(cm[g], st_prev.astype(BF16), preferred_element_type=F32)
        xdec, chunk_decay = [], []
        for tg in range(tiles_per_group):
            t = g * tiles_per_group + tg
            heads = (2 * t, 2 * t + 1)
            acs_b = [_lane_bcast(acs, h) for h in heads]
            acs_pair = jnp.where(first, acs_b[0], acs_b[1])
            dt_pair = jnp.where(first, _lane_bcast(dt, heads[0]), _lane_bcast(dt, heads[1]))
            decay_in = jnp.exp(acs_pair)
            decay_out = jnp.exp(acs_pair[ch - 1:ch, :] - acs_pair)
            xdt = xs[t] * dt_pair
            xdt_b = xdt.astype(BF16)
            halves = []
            for i, h in enumerate(heads):
                seg = acs_b[i] - acs_t[h:h + 1, :]
                gmat = (cb * jnp.exp(jnp.where(tri, seg, -jnp.inf))).astype(BF16)
                halves.append(jnp.dot(gmat, xdt_b, preferred_element_type=F32))
            y_diag = jnp.where(first, halves[0], halves[1])
            y_tiles.append(y_diag + y_off[:, tg * LANES:(tg + 1) * LANES] * decay_in
                           + xs[t] * dskip_ref[:, t * LANES:(t + 1) * LANES])
            xdec.append((xdt * decay_out).astype(BF16))
            chunk_decay.append(decay_in[ch - 1:ch, :])
        xdec_g = jnp.concatenate(xdec, axis=1)
        contrib = lax.dot_general(bm[g], xdec_g, (((0,), (0,)), ((), ())), preferred_element_type=F32)
        st_sc[g] = st_prev * jnp.concatenate(chunk_decay, axis=1) + contrib

    for g in range(SSM_GROUPS):
        yg = jnp.concatenate(y_tiles[g * tiles_per_group:(g + 1) * tiles_per_group], axis=1)
        cols = slice(g * GROUP_WIDTH, (g + 1) * GROUP_WIDTH)
        yg = yg * gz_ref[:, cols]
        o_ref[:, cols] = _rms(yg, ng_ref[:, cols]).astype(o_ref.dtype)


def _ssd(gz, xa, dt, dt_bias, a_log, d_skip, norm_g, batch, seq):
    t = gz.shape[0]
    ch = SSD_CHUNK
    nc = seq // ch
    tok = lambda b, c: (b * nc + c, 0)
    return pl.pallas_call(
        _ssd_body,
        out_shape=jax.ShapeDtypeStruct((t, SSM_WIDTH), BF16),
        grid=(batch, nc),
        in_specs=[
            pl.BlockSpec((ch, SSM_WIDTH), tok),
            pl.BlockSpec((ch, XBC_COLS), tok),
            pl.BlockSpec((ch, LANES), tok),
            _const_spec((1, LANES)),
            _const_spec((1, LANES)),
            _const_spec((1, SSM_WIDTH)),
            _const_spec((1, SSM_WIDTH)),
        ],
        out_specs=pl.BlockSpec((ch, SSM_WIDTH), tok),
        scratch_shapes=[pltpu.VMEM((SSM_GROUPS, SSM_STATE, GROUP_WIDTH), F32)],
        compiler_params=_params("parallel", "arbitrary"),
        name="ssd",
    )(gz, xa, dt, dt_bias, a_log, d_skip, norm_g)


def _outproj_body(x_ref, attn_ref, ssm_ref, wa_ref, ws_ref, g_ref, o_ref):
    mixed = jnp.dot(attn_ref[...], wa_ref[...], preferred_element_type=F32)
    mixed = mixed + jnp.dot(ssm_ref[...], ws_ref[...], preferred_element_type=F32)
    o_ref[...] = x_ref[...] + _rms(mixed, g_ref[...])


def _outproj(x, attn, ssm, wa, ws, g):
    t, d = x.shape
    tm = PROJ_TOKEN_TILE
    return pl.pallas_call(
        _outproj_body,
        out_shape=jax.ShapeDtypeStruct((t, d), F32),
        grid=(t // tm,),
        in_specs=[
            pl.BlockSpec((tm, d), lambda i: (i, 0)),
            pl.BlockSpec((tm, ATTN_WIDTH), lambda i: (i, 0)),
            pl.BlockSpec((tm, SSM_WIDTH), lambda i: (i, 0)),
            _const_spec(wa.shape),
            _const_spec(ws.shape),
            _const_spec((1, d)),
        ],
        out_specs=pl.BlockSpec((tm, d), lambda i: (i, 0)),
        compiler_params=_params("parallel"),
        name="outproj",
    )(x, attn, ssm, wa, ws, g)


def _pad_lanes(row):
    return jnp.pad(row.astype(F32), (0, LANES - row.shape[0]))[None, :]


def _layer(x, p, layer_index, batch, seq):
    lambda_init = 0.8 - 0.6 * math.exp(-0.3 * layer_index)
    row = lambda a: a.astype(F32)[None, :]

    x = _ffn(x, row(p["ffn1_pre_g"]), p["ffn1_w_gate"].astype(BF16), p["ffn1_w_up"].astype(BF16),
             p["ffn1_w_down"].astype(BF16), row(p["ffn1_post_g"]))

    w_in = p["w_in"]
    dt_cols = w_in[:, -SSM_HEADS:]
    w_packed = jnp.concatenate(
        [w_in[:, :-SSM_HEADS], jnp.pad(dt_cols, ((0, 0), (0, LANES - SSM_HEADS)))], axis=1).astype(BF16)
    f32 = lambda a: a.astype(F32)
    q, k, v, gz, xa, dt = _inproj(x, row(p["mix_pre_g"]), w_packed, f32(p["conv_w"]), row(p["conv_b"]), seq)

    lam = jnp.stack([f32(p[n]) for n in ("lambda_q1", "lambda_k1", "lambda_q2", "lambda_k2")])
    g2 = jnp.tile(row(p["attn_subln_g"]), (1, LANES // ATTN_V_DIM))
    attn = _attention(lam, g2, q, k, v, batch, seq, lambda_init)

    d_skip = jnp.repeat(f32(p["d_skip"]), SSM_HEAD_DIM)[None, :]
    ssm = _ssd(gz, xa, dt, _pad_lanes(p["dt_bias"]), _pad_lanes(p["a_log"]), d_skip,
               row(p["ssm_norm_g"]), batch, seq)

    w_out = p["w_out"].astype(BF16)
    x = _outproj(x, attn, ssm, w_out[:ATTN_WIDTH], w_out[ATTN_WIDTH:], row(p["mix_post_g"]))

    return _ffn(x, row(p["ffn2_pre_g"]), p["ffn2_w_gate"].astype(BF16), p["ffn2_w_up"].astype(BF16),
                p["ffn2_w_down"].astype(BF16), row(p["ffn2_post_g"]))


def kernel(x, ffn1_pre_g, ffn1_w_gate, ffn1_w_up, ffn1_w_down, ffn1_post_g, mix_pre_g, w_in, lambda_q1, lambda_k1, lambda_q2, lambda_k2, attn_subln_g, conv_w, conv_b, dt_bias, a_log, d_skip, ssm_norm_g, w_out, mix_post_g, ffn2_pre_g, ffn2_w_gate, ffn2_w_up, ffn2_w_down, ffn2_post_g):
    params = dict(
        ffn1_pre_g=ffn1_pre_g, ffn1_w_gate=ffn1_w_gate, ffn1_w_up=ffn1_w_up, ffn1_w_down=ffn1_w_down,
        ffn1_post_g=ffn1_post_g, mix_pre_g=mix_pre_g, w_in=w_in, lambda_q1=lambda_q1, lambda_k1=lambda_k1,
        lambda_q2=lambda_q2, lambda_k2=lambda_k2, attn_subln_g=attn_subln_g, conv_w=conv_w, conv_b=conv_b,
        dt_bias=dt_bias, a_log=a_log, d_skip=d_skip, ssm_norm_g=ssm_norm_g, w_out=w_out,
        mix_post_g=mix_post_g, ffn2_pre_g=ffn2_pre_g, ffn2_w_gate=ffn2_w_gate, ffn2_w_up=ffn2_w_up,
        ffn2_w_down=ffn2_w_down, ffn2_post_g=ffn2_post_g)
    batch, seq, d = x.shape
    h = x.reshape(batch * seq, d)
    for i in range(ffn1_pre_g.shape[0]):
        h = _layer(h, {name: a[i] for name, a in params.items()}, i, batch, seq)
    return h.reshape(batch, seq, d)
```
